```python
import math
import jax, jax.numpy as jnp
from jax import lax
import numpy as np

D_MODEL = 1024
BATCH = 16
SEQ = 4096
DEPTH = 2
DEC_BATCH = 8
DEC_SEQ = 8192
PAST_LEN = 128

D_MIX = D_MODEL
CONV_W = D_MIX // 2
DN_W = D_MIX - CONV_W
DN_HEADS = 4
DK = DN_W // DN_HEADS
DV = DK
CONV_K = 31
SC_K = 5
CHUNK = 64
D_FF = 256 * ((8 * D_MODEL // 3 + 255) // 256)
ALPHA = (2 * DEPTH) ** 0.25
DN_BETA = (8 * DEPTH) ** -0.25
LN_EPS = 1e-5
NORM_EPS = 1e-6

OFF_CV = 0
OFF_CG = OFF_CV + CONV_W
OFF_Q = OFF_CG + CONV_W
OFF_K = OFF_Q + DN_W
OFF_V = OFF_K + DN_W
OFF_Z = OFF_V + DN_W
OFF_B = OFF_Z + DN_W
OFF_A = OFF_B + 2 * DN_HEADS
D_IN = OFF_A + 2 * DN_HEADS

kernel_name = "hymba_conformer_gdn_encoder"


def layer_norm(x, g, b):
    xf = x.astype(jnp.float32)
    mu = jnp.mean(xf, -1, keepdims=True)
    var = jnp.mean(jnp.square(xf - mu), -1, keepdims=True)
    y = (xf - mu) * lax.rsqrt(var + LN_EPS) * g.astype(jnp.float32) + b.astype(jnp.float32)
    return y.astype(x.dtype)


def l2norm(t):
    return t * lax.rsqrt(jnp.sum(jnp.square(t), -1, keepdims=True) + NORM_EPS)


def depthwise_conv(x, w):
    K, C = w.shape
    pad = K // 2
    return lax.conv_general_dilated(
        x, w[:, None, :].astype(x.dtype), window_strides=(1,), padding=[(pad, pad)],
        dimension_numbers=("NWC", "WIO", "NWC"), feature_group_count=C)


def swiglu(x, wg, wu, wd):
    return (jax.nn.silu(x @ wg) * (x @ wu)) @ wd


def gated_delta_rule_chunked(q, k, v, g, beta):
    B, L, H, _ = q.shape
    N = L // CHUNK

    def chunks(t):
        t = t.reshape((B, N, CHUNK, H) + t.shape[3:])
        return jnp.moveaxis(t, 3, 1)

    q, k, v, g, beta = (chunks(t.astype(jnp.float32)) for t in (q, k, v, g, beta))
    gc = jnp.cumsum(g, axis=-1)
    incl = jnp.tril(jnp.ones((CHUNK, CHUNK), bool))
    strict = jnp.tril(jnp.ones((CHUNK, CHUNK), bool), -1)
    diff = gc[..., :, None] - gc[..., None, :]
    decay = jnp.where(incl, jnp.exp(jnp.where(incl, diff, 0.0)), 0.0)
    kb = k * beta[..., None]
    vb = v * beta[..., None]
    lower = jnp.where(strict, jnp.einsum('bhnid,bhnjd->bhnij', kb, k) * decay, 0.0)
    rhs = jnp.concatenate([vb, kb * jnp.exp(gc)[..., None]], -1)
    sol = lax.linalg.triangular_solve(lower, rhs, left_side=True, lower=True, unit_diagonal=True)
    u, w = sol[..., :DV], sol[..., DV:]
    qk = jnp.einsum('bhnid,bhnjd->bhnij', q, k) * decay
    q_dec = q * jnp.exp(gc)[..., None]
    k_tail = k * jnp.exp(gc[..., -1:] - gc)[..., None]
    g_last = jnp.exp(gc[..., -1])

    def step(S, xs):
        u_c, w_c, qk_c, qd_c, kt_c, gl_c = xs
        v_new = u_c - jnp.einsum('bhck,bhkv->bhcv', w_c, S)
        o = jnp.einsum('bhck,bhkv->bhcv', qd_c, S) + jnp.einsum('bhij,bhjv->bhiv', qk_c, v_new)
        S = S * gl_c[..., None, None] + jnp.einsum('bhck,bhcv->bhkv', kt_c, v_new)
        return S, o

    xs = tuple(jnp.moveaxis(t, 2, 0) for t in (u, w, qk, q_dec, k_tail, g_last))
    S0 = jnp.zeros((B, H, DK, DV), jnp.float32)
    _, o = lax.scan(step, S0, xs)
    return jnp.transpose(o, (1, 0, 3, 2, 4)).reshape(B, L, H, DV)


def token_mixer(h, w_in, conv_w, conv_b, conv_ln_g, conv_ln_b, sconv_w, a_log, dt_bias, o_norm_w, w_out):
    B, L, _ = h.shape
    H = DN_HEADS
    p = h @ w_in
    cv = p[..., OFF_CV:OFF_CG] * jax.nn.sigmoid(p[..., OFF_CG:OFF_Q])
    cv = depthwise_conv(cv, conv_w) + conv_b
    cv = jax.nn.silu(layer_norm(cv, conv_ln_g, conv_ln_b))
    qkv = jax.nn.silu(depthwise_conv(p[..., OFF_Q:OFF_Z], sconv_w)).astype(jnp.float32)
    q = qkv[..., :DN_W].reshape(B, L, H, DK)
    k = qkv[..., DN_W:2 * DN_W].reshape(B, L, H, DK)
    v = qkv[..., 2 * DN_W:].reshape(B, L, H, DV)
    q = l2norm(q) * (DK ** -0.5)
    k = l2norm(k)
    z = p[..., OFF_Z:OFF_B].reshape(B, L, H, DV).astype(jnp.float32)
    beta = jax.nn.sigmoid(p[..., OFF_B:OFF_A].astype(jnp.float32)).reshape(B, L, 2, H)
    a = p[..., OFF_A:D_IN].astype(jnp.float32).reshape(B, L, 2, H)
    g = -jnp.exp(a_log.astype(jnp.float32)) * jax.nn.softplus(a + dt_bias.astype(jnp.float32))
    o_fwd = gated_delta_rule_chunked(q, k, v, g[:, :, 0], beta[:, :, 0])
    flip = lambda t: jnp.flip(t, axis=1)
    o_bwd = flip(gated_delta_rule_chunked(flip(q), flip(k), flip(v), flip(g[:, :, 1]), flip(beta[:, :, 1])))
    o = o_fwd + o_bwd
    o = o * lax.rsqrt(jnp.mean(jnp.square(o), -1, keepdims=True) + NORM_EPS) * o_norm_w.astype(jnp.float32)
    o = (o * jax.nn.silu(z)).reshape(B, L, DN_W).astype(h.dtype)
    mix = jnp.concatenate([cv, o], -1)
    return mix @ w_out


def trunk(x, ffn1_wg, ffn1_wu, ffn1_wd, ln1_g, ln1_b, w_in, conv_w, conv_b, conv_ln_g, conv_ln_b,
          sconv_w, a_log, dt_bias, o_norm_w, w_out, ln2_g, ln2_b, ffn2_wg, ffn2_wu, ffn2_wd, ln3_g, ln3_b):
    for l in range(DEPTH):
        x = layer_norm(ALPHA * x + 0.5 * swiglu(x, ffn1_wg[l], ffn1_wu[l], ffn1_wd[l]), ln1_g[l], ln1_b[l])
        x = layer_norm(ALPHA * x + token_mixer(x, w_in[l], conv_w[l], conv_b[l], conv_ln_g[l], conv_ln_b[l],
                                               sconv_w[l], a_log[l], dt_bias[l], o_norm_w[l], w_out[l]),
                       ln2_g[l], ln2_b[l])
        x = layer_norm(ALPHA * x + 0.5 * swiglu(x, ffn2_wg[l], ffn2_wu[l], ffn2_wd[l]), ln3_g[l], ln3_b[l])
    return x


def setup_inputs(seed: int = 0) -> dict:
    key = jax.random.key(seed)
    ks = iter(jax.random.split(key, 40))
    nrm = lambda shape, s: jax.random.normal(next(ks), shape, jnp.float32) * s
    gain = lambda shape: 1.0 + nrm(shape, 0.02)
    x_prompt = jax.random.normal(next(ks), (BATCH, SEQ, D_MODEL), jnp.float32)
    x_sample = jax.random.normal(next(ks), (DEC_BATCH, DEC_SEQ, D_MODEL), jnp.float32)
    col_scale = jnp.ones((D_IN,), jnp.float32)
    col_scale = col_scale.at[OFF_CV:OFF_CG].set(DN_BETA).at[OFF_V:OFF_Z].set(DN_BETA)
    w_in = nrm((DEPTH, D_MODEL, D_IN), D_MODEL ** -0.5) * col_scale
    a_log = jnp.log(jax.random.uniform(next(ks), (DEPTH, 2, DN_HEADS), jnp.float32, 1.0, 16.0))
    dt = jnp.exp(jax.random.uniform(next(ks), (DEPTH, 2, DN_HEADS), jnp.float32,
                                    math.log(1e-3), math.log(1e-1)))
    dt_bias = dt + jnp.log(-jnp.expm1(-dt))
    return {
        "x_prompt": x_prompt,
        "x_sample": x_sample,
        "ffn1_wg": nrm((DEPTH, D_MODEL, D_FF), D_MODEL ** -0.5),
        "ffn1_wu": nrm((DEPTH, D_MODEL, D_FF), D_MODEL ** -0.5),
        "ffn1_wd": nrm((DEPTH, D_FF, D_MODEL), D_FF ** -0.5 * DN_BETA),
        "ln1_g": gain((DEPTH, D_MODEL)),
        "ln1_b": nrm((DEPTH, D_MODEL), 0.02),
        "w_in": w_in,
        "conv_w": nrm((DEPTH, CONV_K, CONV_W), CONV_K ** -0.5),
        "conv_b": nrm((DEPTH, CONV_W), 0.02),
        "conv_ln_g": gain((DEPTH, CONV_W)),
        "conv_ln_b": nrm((DEPTH, CONV_W), 0.02),
        "sconv_w": nrm((DEPTH, SC_K, 3 * DN_W), SC_K ** -0.5),
        "a_log": a_log,
        "dt_bias": dt_bias,
        "o_norm_w": gain((DEPTH, DV)),
        "w_out": nrm((DEPTH, D_MIX, D_MODEL), D_MIX ** -0.5 * DN_BETA),
        "ln2_g": gain((DEPTH, D_MODEL)),
        "ln2_b": nrm((DEPTH, D_MODEL), 0.02),
        "ffn2_wg": nrm((DEPTH, D_MODEL, D_FF), D_MODEL ** -0.5),
        "ffn2_wu": nrm((DEPTH, D_MODEL, D_FF), D_MODEL ** -0.5),
        "ffn2_wd": nrm((DEPTH, D_FF, D_MODEL), D_FF ** -0.5 * DN_BETA),
        "ln3_g": gain((DEPTH, D_MODEL)),
        "ln3_b": nrm((DEPTH, D_MODEL), 0.02),
    }


def reference(x_prompt, x_sample, ffn1_wg, ffn1_wu, ffn1_wd, ln1_g, ln1_b, w_in, conv_w, conv_b,
              conv_ln_g, conv_ln_b, sconv_w, a_log, dt_bias, o_norm_w, w_out, ln2_g, ln2_b,
              ffn2_wg, ffn2_wu, ffn2_wd, ln3_g, ln3_b):
    y_prompt = trunk(x_prompt, ffn1_wg, ffn1_wu, ffn1_wd, ln1_g, ln1_b, w_in, conv_w, conv_b, conv_ln_g,
                     conv_ln_b, sconv_w, a_log, dt_bias, o_norm_w, w_out, ln2_g, ln2_b,
                     ffn2_wg, ffn2_wu, ffn2_wd, ln3_g, ln3_b)
    y_sample = trunk(x_sample, ffn1_wg, ffn1_wu, ffn1_wd, ln1_g, ln1_b, w_in, conv_w, conv_b, conv_ln_g,
                     conv_ln_b, sconv_w, a_log, dt_bias, o_norm_w, w_out, ln2_g, ln2_b,
                     ffn2_wg, ffn2_wu, ffn2_wd, ln3_g, ln3_b)
    return (y_prompt, y_sample)
```

```python
import functools

import jax
import jax.numpy as jnp
from jax import lax
from jax.experimental import pallas as pl
from jax.experimental.pallas import tpu as pltpu

F32 = jnp.float32
BF16 = jnp.bfloat16

DEPTH = 2
D_MODEL = 1024
CONV_W = 512
DN_W = 512
HEADS = 4
DK = 128
CONV_K = 31
SC_K = 5
D_FF = 2816
ALPHA = (2 * DEPTH) ** 0.25
LN_EPS = 1e-5
NORM_EPS = 1e-6

SCAN_CHUNK = 128
HALO = 16
TOKEN_TILE = 512
SEQ_TILE = 512
CONV_ROWS = 32
VMEM_LIMIT = 56 * 1024 * 1024


def _dot(a, b):
    return jnp.dot(a, b, preferred_element_type=F32)


def _bdot(a, b):
    return jnp.dot(a.astype(BF16), b.astype(BF16), preferred_element_type=F32)


def _silu(x):
    return x * jax.nn.sigmoid(x)


def _layer_norm(r, g, b):
    mu = jnp.mean(r, -1, keepdims=True)
    c = r - mu
    var = jnp.mean(c * c, -1, keepdims=True)
    return c * lax.rsqrt(var + LN_EPS) * g + b


def _swiglu(x, wg_ref, wu_ref, wd_ref):
    xb = x.astype(BF16)
    hg = _dot(xb, wg_ref[...])
    hu = _dot(xb, wu_ref[...])
    a = (_silu(hg) * hu).astype(BF16)
    return _dot(a, wd_ref[...])


def _ffn_ln_kernel(x_ref, wg_ref, wu_ref, wd_ref, g_ref, b_ref, o_ref):
    x = x_ref[...]
    r = ALPHA * x + 0.5 * _swiglu(x, wg_ref, wu_ref, wd_ref)
    o_ref[...] = _layer_norm(r, g_ref[...], b_ref[...])


def _mixout_ffn_kernel(x_ref, cv_ref, og_ref, woc_ref, wod_ref, g2_ref, b2_ref,
                       wg_ref, wu_ref, wd_ref, g3_ref, b3_ref, o_ref):
    x = x_ref[...]
    m = _dot(cv_ref[...], woc_ref[...]) + _dot(og_ref[...], wod_ref[...])
    x2 = _layer_norm(ALPHA * x + m, g2_ref[...], b2_ref[...])
    r = ALPHA * x2 + 0.5 * _swiglu(x2, wg_ref, wu_ref, wd_ref)
    o_ref[...] = _layer_norm(r, g3_ref[...], b3_ref[...])


def _const_spec(shape):
    nd = len(shape)
    return pl.BlockSpec(shape, lambda *_: (0,) * nd, pipeline_mode=pl.Buffered(1))


def _row_spec(tm, width):
    return pl.BlockSpec((tm, width), lambda i: (i, 0))


def _ffn_ln(x, wg, wu, wd, g, b):
    n = x.shape[0]
    tm = min(TOKEN_TILE, n)
    return pl.pallas_call(
        _ffn_ln_kernel,
        grid=(n // tm,),
        in_specs=[_row_spec(tm, D_MODEL), _const_spec(wg.shape), _const_spec(wu.shape),
                  _const_spec(wd.shape), _const_spec(g.shape), _const_spec(b.shape)],
        out_specs=_row_spec(tm, D_MODEL),
        out_shape=jax.ShapeDtypeStruct((n, D_MODEL), F32),
        compiler_params=pltpu.CompilerParams(dimension_semantics=("parallel",),
                                             vmem_limit_bytes=VMEM_LIMIT),
        name="ffn_ln",
    )(x, wg, wu, wd, g, b)


def _mixout_ffn(x, cv, og, woc, wod, g2, b2, wg, wu, wd, g3, b3):
    n = x.shape[0]
    tm = min(TOKEN_TILE, n)
    consts = (woc, wod, g2, b2, wg, wu, wd, g3, b3)
    return pl.pallas_call(
        _mixout_ffn_kernel,
        grid=(n // tm,),
        in_specs=[_row_spec(tm, D_MODEL), _row_spec(tm, CONV_W), _row_spec(tm, DN_W)]
                 + [_const_spec(c.shape) for c in consts],
        out_specs=_row_spec(tm, D_MODEL),
        out_shape=jax.ShapeDtypeStruct((n, D_MODEL), F32),
        compiler_params=pltpu.CompilerParams(dimension_semantics=("parallel",),
                                             vmem_limit_bytes=VMEM_LIMIT),
        name="mixout_ffn",
    )(x, cv, og, *consts)


def _mixer_in_kernel(xp_ref, x_ref, xn_ref, wcvg_ref, wqkv_ref, wz_ref, wbaT_ref,
                     convw_ref, convb_ref, clng_ref, clnb_ref, sconvw_ref, alog_ref, dtb_ref,
                     cv_ref, q_ref, k_ref, v_ref, kT_ref, gate_ref, gbT_ref, gbC_ref,
                     xh_ref, glu_ref, pq_ref, *, tile, n_tiles):
    t = pl.program_id(1)
    xh_ref[0:HALO, :] = jnp.where(t > 0, xp_ref[0], 0.0).astype(BF16)
    xh_ref[HALO:HALO + tile, :] = x_ref[0].astype(BF16)
    xh_ref[HALO + tile:, :] = jnp.where(t < n_tiles - 1, xn_ref[0], 0.0).astype(BF16)

    xh = xh_ref[...]
    pc = _dot(xh, wcvg_ref[...])
    glu_ref[...] = pc[:, :CONV_W] * jax.nn.sigmoid(pc[:, CONV_W:])
    pq_ref[...] = _dot(xh, wqkv_ref[...])

    convb = convb_ref[...]
    clng = clng_ref[...]
    clnb = clnb_ref[...]
    off = HALO - CONV_K // 2
    for r in range(tile // CONV_ROWS):
        base = r * CONV_ROWS + off
        acc = convw_ref[0:1, :] * glu_ref[base:base + CONV_ROWS, :]
        for kk in range(1, CONV_K):
            acc = acc + convw_ref[kk:kk + 1, :] * glu_ref[base + kk:base + kk + CONV_ROWS, :]
        c = _silu(_layer_norm(acc + convb, clng, clnb))
        cv_ref[0, r * CONV_ROWS:(r + 1) * CONV_ROWS, :] = c.astype(cv_ref.dtype)

    off = HALO - SC_K // 2
    rows = 2 * CONV_ROWS
    for j in range(3 * HEADS):
        cols = slice(j * DK, (j + 1) * DK)
        for r in range(tile // rows):
            base = r * rows + off
            acc = sconvw_ref[0:1, cols] * pq_ref[base:base + rows, cols]
            for kk in range(1, SC_K):
                acc = acc + sconvw_ref[kk:kk + 1, cols] * pq_ref[base + kk:base + kk + rows, cols]
            y = _silu(acc)
            out_rows = slice(r * rows, (r + 1) * rows)
            if j < HEADS:
                y = y * lax.rsqrt(jnp.sum(y * y, -1, keepdims=True) + NORM_EPS) * (DK ** -0.5)
                q_ref[0, out_rows, cols] = y
            elif j < 2 * HEADS:
                y = y * lax.rsqrt(jnp.sum(y * y, -1, keepdims=True) + NORM_EPS)
                k_ref[0, out_rows, slice((j - HEADS) * DK, (j - HEADS + 1) * DK)] = y
            else:
                v_ref[0, out_rows, slice((j - 2 * HEADS) * DK, (j - 2 * HEADS + 1) * DK)] = y

    n_chunks = tile // SCAN_CHUNK
    for h in range(HEADS):
        cols = slice(h * DK, (h + 1) * DK)
        for c in range(n_chunks):
            rws = slice(c * SCAN_CHUNK, (c + 1) * SCAN_CHUNK)
            kT_ref[0, cols, rws] = k_ref[0, rws, cols].T

    xm = xh_ref[HALO:HALO + tile, :]
    gate_ref[0] = _silu(_dot(xm, wz_ref[...]))

    pb = lax.dot_general(wbaT_ref[...], xm, (((1,), (1,)), ((), ())),
                         preferred_element_type=F32)
    beta = jax.nn.sigmoid(pb[0:2 * HEADS, :])
    a = pb[2 * HEADS:, :] + dtb_ref[...]
    softplus = jnp.maximum(a, 0.0) + jnp.log1p(jnp.exp(-jnp.abs(a)))
    g = -jnp.exp(alog_ref[...]) * softplus
    lane = lax.broadcasted_iota(jnp.int32, (2 * HEADS, SCAN_CHUNK), 1)
    fwd_row = lax.broadcasted_iota(jnp.int32, (2 * HEADS, SCAN_CHUNK), 0) < HEADS
    zpad = jnp.zeros((SCAN_CHUNK - 4 * HEADS, SCAN_CHUNK), F32)
    for c in range(n_chunks):
        cl = slice(c * SCAN_CHUNK, (c + 1) * SCAN_CHUNK)
        x = g[:, cl]
        s = 1
        while s < SCAN_CHUNK:
            pre = jnp.where(lane >= s, pltpu.roll(x, s, 1), 0.0)
            suf = jnp.where(lane < SCAN_CHUNK - s, pltpu.roll(x, SCAN_CHUNK - s, 1), 0.0)
            x = x + jnp.where(fwd_row, pre, suf)
            s *= 2
        rows16 = jnp.concatenate([beta[:, cl], x], axis=0)
        gbT_ref[0, :, cl] = rows16
        gbC_ref[0, cl, :] = jnp.concatenate([rows16, zpad], axis=0).T


def _mixer_in(x, wcvg, wqkv, wz, wbaT, convw, convb, clng, clnb, sconvw, alog, dtb):
    B, L, D = x.shape
    tile = min(SEQ_TILE, L)
    n_tiles = L // tile
    hb = tile // HALO
    n_hb = L // HALO
    consts = (wcvg, wqkv, wz, wbaT, convw, convb, clng, clnb, sconvw, alog, dtb)
    seq = lambda w, dt: jax.ShapeDtypeStruct((B, L, w), dt)
    seq_spec = lambda w: pl.BlockSpec((1, tile, w), lambda b, t: (b, t, 0))
    return pl.pallas_call(
        functools.partial(_mixer_in_kernel, tile=tile, n_tiles=n_tiles),
        grid=(B, n_tiles),
        in_specs=[pl.BlockSpec((1, HALO, D), lambda b, t: (b, jnp.maximum(t * hb - 1, 0), 0)),
                  pl.BlockSpec((1, tile, D), lambda b, t: (b, t, 0)),
                  pl.BlockSpec((1, HALO, D), lambda b, t: (b, jnp.minimum((t + 1) * hb, n_hb - 1), 0))]
                 + [_const_spec(c.shape) for c in consts],
        out_specs=[seq_spec(CONV_W), seq_spec(DN_W), seq_spec(DN_W), seq_spec(DN_W),
                   pl.BlockSpec((1, DN_W, tile), lambda b, t: (b, 0, t)),
                   seq_spec(DN_W),
                   pl.BlockSpec((1, 4 * HEADS, tile), lambda b, t: (b, 0, t)),
                   seq_spec(SCAN_CHUNK)],
        out_shape=[seq(CONV_W, BF16), seq(DN_W, F32), seq(DN_W, F32), seq(DN_W, F32),
                   jax.ShapeDtypeStruct((B, DN_W, L), F32),
                   seq(DN_W, F32),
                   jax.ShapeDtypeStruct((B, 4 * HEADS, L), F32),
                   seq(SCAN_CHUNK, F32)],
        scratch_shapes=[pltpu.VMEM((tile + 2 * HALO, D), BF16),
                        pltpu.VMEM((tile + 2 * HALO, CONV_W), F32),
                        pltpu.VMEM((tile + 2 * HALO, 3 * DN_W), F32)],
        compiler_params=pltpu.CompilerParams(dimension_semantics=("parallel", "parallel"),
                                             vmem_limit_bytes=VMEM_LIMIT),
        name="mixer_in",
    )(x, x, x, *consts)


def _unit_tri_inverse(a, ii, jj):
    eye = (ii == jj).astype(F32)
    n = jnp.where((ii >> 3) == (jj >> 3), -a, 0.0)
    t = eye + n
    p = _bdot(n, n)
    t = t + _bdot(p, t)
    p = _bdot(p, p)
    t = t + _bdot(p, t)
    sh = 3
    while (1 << sh) < SCAN_CHUNK:
        e = jnp.where(((ii >> (sh + 1)) == (jj >> (sh + 1))) & ((ii >> sh) != (jj >> sh)), a, 0.0)
        t = t - _bdot(t, _bdot(e, t))
        sh += 1
    return t


def _delta_scan_kernel(*refs, tile, reverse, final):
    if final:
        (q_ref, k_ref, v_ref, kT_ref, gbT_ref, gbC_ref, ofwd_ref, gate_ref, onw_ref,
         o_ref, s_ref) = refs
    else:
        q_ref, k_ref, v_ref, kT_ref, gbT_ref, gbC_ref, o_ref, s_ref = refs
    C = SCAN_CHUNK
    d = 1 if reverse else 0

    @pl.when(pl.program_id(1) == 0)
    def _():
        s_ref[...] = jnp.zeros_like(s_ref)

    ii = lax.broadcasted_iota(jnp.int32, (C, C), 0)
    jj = lax.broadcasted_iota(jnp.int32, (C, C), 1)
    incl = (ii <= jj) if reverse else (ii >= jj)
    strict = (ii < jj) if reverse else (ii > jj)
    last = 0 if reverse else C - 1

    n_chunks = tile // C
    order = range(n_chunks - 1, -1, -1) if reverse else range(n_chunks)
    for c in order:
        rws = slice(c * C, (c + 1) * C)
        colv = gbC_ref[0, rws, :]
        rowv = gbT_ref[0, :, rws]
        for h in range(HEADS):
            cols = slice(h * DK, (h + 1) * DK)
            q = q_ref[0, rws, cols]
            k = k_ref[0, rws, cols]
            v = v_ref[0, rws, cols]
            kT = kT_ref[0, cols, rws]
            ib = d * HEADS + h
            ig = 2 * HEADS + ib
            beta_c = colv[:, ib:ib + 1]
            gc_c = colv[:, ig:ig + 1]
            gc_r = rowv[ig:ig + 1, :]
            g_tot = gc_r[:, last:last + 1]

            decay = jnp.where(incl, jnp.exp(jnp.where(incl, gc_c - gc_r, 0.0)), 0.0)
            qkk = _bdot(jnp.concatenate([q, k], axis=0), kT)
            qk = qkk[:C] * decay
            a = jnp.where(strict, qkk[C:] * decay * beta_c, 0.0)
            t = _unit_tri_inverse(a, ii, jj)

            eg_c = jnp.exp(gc_c)
            rhs = jnp.concatenate([v * beta_c, k * (beta_c * eg_c)], axis=1)
            sol = _bdot(t, rhs)
            u = sol[:, :DK]
            w = sol[:, DK:]

            s = s_ref[h]
            ws = _bdot(jnp.concatenate([w, q * eg_c], axis=0), s)
            v_new = u - ws[:C]
            o = ws[C:] + _bdot(qk, v_new)
            k_tail_t = kT * jnp.exp(g_tot - gc_r)
            s_ref[h] = s * jnp.exp(g_tot) + _bdot(k_tail_t, v_new)

            if final:
                o = o + ofwd_ref[0, rws, cols]
                o = o * lax.rsqrt(jnp.mean(o * o, -1, keepdims=True) + NORM_EPS) * onw_ref[...]
                o_ref[0, rws, cols] = (o * gate_ref[0, rws, cols]).astype(o_ref.dtype)
            else:
                o_ref[0, rws, cols] = o


def _delta_scan(q, k, v, kT, gbT, gbC, *, reverse, final_inputs=None):
    B, L, _ = q.shape
    tile = min(SEQ_TILE, L)
    n_tiles = L // tile
    final = final_inputs is not None
    tix = (lambda t: n_tiles - 1 - t) if reverse else (lambda t: t)
    seq_spec = lambda w: pl.BlockSpec((1, tile, w), lambda b, t: (b, tix(t), 0))
    in_specs = [seq_spec(DN_W), seq_spec(DN_W), seq_spec(DN_W),
                pl.BlockSpec((1, DN_W, tile), lambda b, t: (b, 0, tix(t))),
                pl.BlockSpec((1, 4 * HEADS, tile), lambda b, t: (b, 0, tix(t))),
                seq_spec(SCAN_CHUNK)]
    args = [q, k, v, kT, gbT, gbC]
    if final:
        ofwd, gate, onw = final_inputs
        in_specs += [seq_spec(DN_W), seq_spec(DN_W), _const_spec(onw.shape)]
        args += [ofwd, gate, onw]
    return pl.pallas_call(
        functools.partial(_delta_scan_kernel, tile=tile, reverse=reverse, final=final),
        grid=(B, n_tiles),
        in_specs=in_specs,
        out_specs=seq_spec(DN_W),
        out_shape=jax.ShapeDtypeStruct((B, L, DN_W), BF16 if final else F32),
        scratch_shapes=[pltpu.VMEM((HEADS, DK, DK), F32)],
        compiler_params=pltpu.CompilerParams(dimension_semantics=("parallel", "arbitrary"),
                                             vmem_limit_bytes=VMEM_LIMIT),
        name="delta_scan_bwd" if reverse else "delta_scan_fwd",
    )(*args)


def _layer_params(l, ffn1_wg, ffn1_wu, ffn1_wd, ln1_g, ln1_b, w_in, conv_w, conv_b, conv_ln_g,
                  conv_ln_b, sconv_w, a_log, dt_bias, o_norm_w, w_out, ln2_g, ln2_b,
                  ffn2_wg, ffn2_wu, ffn2_wd, ln3_g, ln3_b):
    row = lambda a: a[l].reshape(1, -1).astype(F32)
    col = lambda a: a[l].reshape(-1, 1).astype(F32)
    wi = w_in[l].astype(BF16)
    o_q = 2 * CONV_W
    o_z = o_q + 3 * DN_W
    o_b = o_z + DN_W
    wo = w_out[l].astype(BF16)
    return dict(
        ffn1=(ffn1_wg[l].astype(BF16), ffn1_wu[l].astype(BF16), ffn1_wd[l].astype(BF16),
              row(ln1_g), row(ln1_b)),
        mixer_in=(wi[:, :o_q], wi[:, o_q:o_z], wi[:, o_z:o_b], wi[:, o_b:].T,
                  conv_w[l].astype(F32), row(conv_b), row(conv_ln_g), row(conv_ln_b),
                  sconv_w[l].astype(F32), col(a_log), col(dt_bias)),
        onw=row(o_norm_w),
        mixout=(wo[:CONV_W], wo[CONV_W:], row(ln2_g), row(ln2_b),
                ffn2_wg[l].astype(BF16), ffn2_wu[l].astype(BF16), ffn2_wd[l].astype(BF16),
                row(ln3_g), row(ln3_b)),
    )


def _trunk(x, layers):
    B, L, D = x.shape
    for p in layers:
        x1 = _ffn_ln(x.reshape(B * L, D), *p["ffn1"])
        cv, q, k, v, kT, gate, gbT, gbC = _mixer_in(x1.reshape(B, L, D), *p["mixer_in"])
        o_fwd = _delta_scan(q, k, v, kT, gbT, gbC, reverse=False)
        og = _delta_scan(q, k, v, kT, gbT, gbC, reverse=True, final_inputs=(o_fwd, gate, p["onw"]))
        x = _mixout_ffn(x1, cv.reshape(B * L, CONV_W), og.reshape(B * L, DN_W), *p["mixout"])
        x = x.reshape(B, L, D)
    return x


def kernel(x_prompt, x_sample, ffn1_wg, ffn1_wu, ffn1_wd, ln1_g, ln1_b, w_in, conv_w, conv_b, conv_ln_g, conv_ln_b, sconv_w, a_log, dt_bias, o_norm_w, w_out, ln2_g, ln2_b, ffn2_wg, ffn2_wu, ffn2_wd, ln3_g, ln3_b):
    weights = (ffn1_wg, ffn1_wu, ffn1_wd, ln1_g, ln1_b, w_in, conv_w, conv_b, conv_ln_g, conv_ln_b,
               sconv_w, a_log, dt_bias, o_norm_w, w_out, ln2_g, ln2_b, ffn2_wg, ffn2_wu, ffn2_wd,
               ln3_g, ln3_b)
    layers = [_layer_params(l, *weights) for l in range(DEPTH)]
    return (_trunk(x_prompt, layers), _trunk(x_sample, layers))
```

```python
import functools

import jax
import jax.numpy as jnp
from jax import lax
from jax.experimental import pallas as pl
from jax.experimental.pallas import tpu as pltpu

F32 = jnp.float32
BF16 = jnp.bfloat16

DEPTH = 2
D_MODEL = 1024
CONV_W = 512
DN_W = 512
HEADS = 4
DK = 128
CONV_K = 31
SC_K = 5
D_FF = 2816
ALPHA = (2 * DEPTH) ** 0.25
LN_EPS = 1e-5
NORM_EPS = 1e-6

SCAN_CHUNK = 128
SUBLANES = 8
HALO = 16
TOKEN_TILE = 512
SEQ_TILE = 512
CONV_ROWS = 32
VMEM_LIMIT = 56 * 1024 * 1024


def _dot(a, b):
    return jnp.dot(a, b, preferred_element_type=F32)


def _bdot(a, b):
    return jnp.dot(a.astype(BF16), b.astype(BF16), preferred_element_type=F32)


def _silu(x):
    return x * jax.nn.sigmoid(x)


def _layer_norm(r, g, b):
    mu = jnp.mean(r, -1, keepdims=True)
    c = r - mu
    var = jnp.mean(c * c, -1, keepdims=True)
    return c * lax.rsqrt(var + LN_EPS) * g + b


def _swiglu(x, wg_ref, wu_ref, wd_ref):
    xb = x.astype(BF16)
    hg = _dot(xb, wg_ref[...])
    hu = _dot(xb, wu_ref[...])
    a = (_silu(hg) * hu).astype(BF16)
    return _dot(a, wd_ref[...])


def _ffn_ln_kernel(x_ref, wg_ref, wu_ref, wd_ref, g_ref, b_ref, o_ref):
    x = x_ref[...]
    r = ALPHA * x + 0.5 * _swiglu(x, wg_ref, wu_ref, wd_ref)
    o_ref[...] = _layer_norm(r, g_ref[...], b_ref[...])


def _mixout_ffn_kernel(x_ref, cv_ref, og_ref, woc_ref, wod_ref, g2_ref, b2_ref,
                       wg_ref, wu_ref, wd_ref, g3_ref, b3_ref, o_ref):
    x = x_ref[...]
    m = _dot(cv_ref[...], woc_ref[...]) + _dot(og_ref[...], wod_ref[...])
    x2 = _layer_norm(ALPHA * x + m, g2_ref[...], b2_ref[...])
    r = ALPHA * x2 + 0.5 * _swiglu(x2, wg_ref, wu_ref, wd_ref)
    o_ref[...] = _layer_norm(r, g3_ref[...], b3_ref[...])


def _const_spec(shape):
    nd = len(shape)
    return pl.BlockSpec(shape, lambda *_: (0,) * nd, pipeline_mode=pl.Buffered(1))


def _row_spec(tm, width):
    return pl.BlockSpec((tm, width), lambda i: (i, 0))


def _ffn_ln(x, wg, wu, wd, g, b):
    n = x.shape[0]
    tm = min(TOKEN_TILE, n)
    return pl.pallas_call(
        _ffn_ln_kernel,
        grid=(n // tm,),
        in_specs=[_row_spec(tm, D_MODEL), _const_spec(wg.shape), _const_spec(wu.shape),
                  _const_spec(wd.shape), _const_spec(g.shape), _const_spec(b.shape)],
        out_specs=_row_spec(tm, D_MODEL),
        out_shape=jax.ShapeDtypeStruct((n, D_MODEL), F32),
        compiler_params=pltpu.CompilerParams(dimension_semantics=("parallel",),
                                             vmem_limit_bytes=VMEM_LIMIT),
        name="ffn_ln",
    )(x, wg, wu, wd, g, b)


def _mixout_ffn(x, cv, og, woc, wod, g2, b2, wg, wu, wd, g3, b3):
    n = x.shape[0]
    tm = min(TOKEN_TILE, n)
    consts = (woc, wod, g2, b2, wg, wu, wd, g3, b3)
    return pl.pallas_call(
        _mixout_ffn_kernel,
        grid=(n // tm,),
        in_specs=[_row_spec(tm, D_MODEL), _row_spec(tm, CONV_W), _row_spec(tm, DN_W)]
                 + [_const_spec(c.shape) for c in consts],
        out_specs=_row_spec(tm, D_MODEL),
        out_shape=jax.ShapeDtypeStruct((n, D_MODEL), F32),
        compiler_params=pltpu.CompilerParams(dimension_semantics=("parallel",),
                                             vmem_limit_bytes=VMEM_LIMIT),
        name="mixout_ffn",
    )(x, cv, og, *consts)


def _mixer_in_kernel(xp_ref, x_ref, xn_ref, wcvg_ref, wqkv_ref, wz_ref, wbaT_ref,
                     convw_ref, convb_ref, clng_ref, clnb_ref, sconvw_ref, alog_ref, dtb_ref,
                     cv_ref, q_ref, k_ref, v_ref, kT_ref, gate_ref, gbT_ref, gbC_ref,
                     xh_ref, glu_ref, sh_ref, pq_ref, *, tile, n_tiles):
    t = pl.program_id(1)
    xh_ref[0:HALO, :] = jnp.where(t > 0, xp_ref[0], 0.0).astype(BF16)
    xh_ref[HALO:HALO + tile, :] = x_ref[0].astype(BF16)
    xh_ref[HALO + tile:, :] = jnp.where(t < n_tiles - 1, xn_ref[0], 0.0).astype(BF16)

    xh = xh_ref[...]
    pc = _dot(xh, wcvg_ref[...])
    glu_ref[...] = pc[:, :CONV_W] * jax.nn.sigmoid(pc[:, CONV_W:])
    pq_ref[...] = _dot(xh, wqkv_ref[...])

    convb = convb_ref[...]
    clng = clng_ref[...]
    clnb = clnb_ref[...]
    off = HALO - CONV_K // 2
    sh_rows = tile + 2 * HALO - SUBLANES
    for s in range(1, SUBLANES):
        sh_ref[s - 1, 0:sh_rows, :] = glu_ref[s:s + sh_rows, :]
    for r in range(tile // CONV_ROWS):
        acc = None
        for kk in range(CONV_K):
            i, s = divmod(off + kk, SUBLANES)
            lo = r * CONV_ROWS + SUBLANES * i
            src = glu_ref[lo:lo + CONV_ROWS, :] if s == 0 else sh_ref[s - 1, lo:lo + CONV_ROWS, :]
            term = convw_ref[kk:kk + 1, :] * src
            acc = term if acc is None else acc + term
        c = _silu(_layer_norm(acc + convb, clng, clnb))
        cv_ref[0, r * CONV_ROWS:(r + 1) * CONV_ROWS, :] = c.astype(cv_ref.dtype)

    off = HALO - SC_K // 2
    rows = 2 * CONV_ROWS
    for j in range(3 * HEADS):
        cols = slice(j * DK, (j + 1) * DK)
        for r in range(tile // rows):
            base = r * rows + off
            acc = sconvw_ref[0:1, cols] * pq_ref[base:base + rows, cols]
            for kk in range(1, SC_K):
                acc = acc + sconvw_ref[kk:kk + 1, cols] * pq_ref[base + kk:base + kk + rows, cols]
            y = _silu(acc)
            out_rows = slice(r * rows, (r + 1) * rows)
            if j < HEADS:
                y = y * lax.rsqrt(jnp.sum(y * y, -1, keepdims=True) + NORM_EPS) * (DK ** -0.5)
                q_ref[0, out_rows, cols] = y
            elif j < 2 * HEADS:
                y = y * lax.rsqrt(jnp.sum(y * y, -1, keepdims=True) + NORM_EPS)
                k_ref[0, out_rows, slice((j - HEADS) * DK, (j - HEADS + 1) * DK)] = y
            else:
                v_ref[0, out_rows, slice((j - 2 * HEADS) * DK, (j - 2 * HEADS + 1) * DK)] = y

    n_chunks = tile // SCAN_CHUNK
    for h in range(HEADS):
        cols = slice(h * DK, (h + 1) * DK)
        for c in range(n_chunks):
            rws = slice(c * SCAN_CHUNK, (c + 1) * SCAN_CHUNK)
            kT_ref[0, cols, rws] = k_ref[0, rws, cols].T

    xm = xh_ref[HALO:HALO + tile, :]
    gate_ref[0] = _silu(_dot(xm, wz_ref[...]))

    pb = lax.dot_general(wbaT_ref[...], xm, (((1,), (1,)), ((), ())),
                         preferred_element_type=F32)
    beta = jax.nn.sigmoid(pb[0:2 * HEADS, :])
    a = pb[2 * HEADS:, :] + dtb_ref[...]
    softplus = jnp.maximum(a, 0.0) + jnp.log1p(jnp.exp(-jnp.abs(a)))
    g = -jnp.exp(alog_ref[...]) * softplus
    lane = lax.broadcasted_iota(jnp.int32, (2 * HEADS, SCAN_CHUNK), 1)
    fwd_row = lax.broadcasted_iota(jnp.int32, (2 * HEADS, SCAN_CHUNK), 0) < HEADS
    zpad = jnp.zeros((SCAN_CHUNK - 4 * HEADS, SCAN_CHUNK), F32)
    for c in range(n_chunks):
        cl = slice(c * SCAN_CHUNK, (c + 1) * SCAN_CHUNK)
        x = g[:, cl]
        s = 1
        while s < SCAN_CHUNK:
            pre = jnp.where(lane >= s, pltpu.roll(x, s, 1), 0.0)
            suf = jnp.where(lane < SCAN_CHUNK - s, pltpu.roll(x, SCAN_CHUNK - s, 1), 0.0)
            x = x + jnp.where(fwd_row, pre, suf)
            s *= 2
        rows16 = jnp.concatenate([beta[:, cl], x], axis=0)
        gbT_ref[0, :, cl] = rows16
        gbC_ref[0, cl, :] = jnp.concatenate([rows16, zpad], axis=0).T


def _mixer_in(x, wcvg, wqkv, wz, wbaT, convw, convb, clng, clnb, sconvw, alog, dtb):
    B, L, D = x.shape
    tile = min(SEQ_TILE, L)
    n_tiles = L // tile
    hb = tile // HALO
    n_hb = L // HALO
    consts = (wcvg, wqkv, wz, wbaT, convw, convb, clng, clnb, sconvw, alog, dtb)
    seq = lambda w, dt: jax.ShapeDtypeStruct((B, L, w), dt)
    seq_spec = lambda w: pl.BlockSpec((1, tile, w), lambda b, t: (b, t, 0))
    return pl.pallas_call(
        functools.partial(_mixer_in_kernel, tile=tile, n_tiles=n_tiles),
        grid=(B, n_tiles),
        in_specs=[pl.BlockSpec((1, HALO, D), lambda b, t: (b, jnp.maximum(t * hb - 1, 0), 0)),
                  pl.BlockSpec((1, tile, D), lambda b, t: (b, t, 0)),
                  pl.BlockSpec((1, HALO, D), lambda b, t: (b, jnp.minimum((t + 1) * hb, n_hb - 1), 0))]
                 + [_const_spec(c.shape) for c in consts],
        out_specs=[seq_spec(CONV_W), seq_spec(DN_W), seq_spec(DN_W), seq_spec(DN_W),
                   pl.BlockSpec((1, DN_W, tile), lambda b, t: (b, 0, t)),
                   seq_spec(DN_W),
                   pl.BlockSpec((1, 4 * HEADS, tile), lambda b, t: (b, 0, t)),
                   seq_spec(SCAN_CHUNK)],
        out_shape=[seq(CONV_W, BF16), seq(DN_W, F32), seq(DN_W, F32), seq(DN_W, F32),
                   jax.ShapeDtypeStruct((B, DN_W, L), F32),
                   seq(DN_W, F32),
                   jax.ShapeDtypeStruct((B, 4 * HEADS, L), F32),
                   seq(SCAN_CHUNK, F32)],
        scratch_shapes=[pltpu.VMEM((tile + 2 * HALO, D), BF16),
                        pltpu.VMEM((tile + 2 * HALO, CONV_W), F32),
                        pltpu.VMEM((SUBLANES - 1, tile + 2 * HALO, CONV_W), F32),
                        pltpu.VMEM((tile + 2 * HALO, 3 * DN_W), F32)],
        compiler_params=pltpu.CompilerParams(dimension_semantics=("parallel", "parallel"),
                                             vmem_limit_bytes=VMEM_LIMIT),
        name="mixer_in",
    )(x, x, x, *consts)


def _unit_tri_inverses(a_list, ii, jj):
    eye = (ii == jj).astype(F32)
    blk = (ii >> 3) == (jj >> 3)
    ns = [jnp.where(blk, -a, 0.0) for a in a_list]
    ts = [eye + n for n in ns]
    ps = [_bdot(n, n) for n in ns]
    ts = [t + _bdot(p, t) for p, t in zip(ps, ts)]
    ps = [_bdot(p, p) for p in ps]
    ts = [t + _bdot(p, t) for p, t in zip(ps, ts)]
    sh = 3
    while (1 << sh) < SCAN_CHUNK:
        m = ((ii >> (sh + 1)) == (jj >> (sh + 1))) & ((ii >> sh) != (jj >> sh))
        ets = [_bdot(jnp.where(m, a, 0.0), t) for a, t in zip(a_list, ts)]
        ts = [t - _bdot(t, et) for t, et in zip(ts, ets)]
        sh += 1
    return ts


def _delta_scan_kernel(*refs, tile, reverse, final):
    if final:
        (q_ref, k_ref, v_ref, kT_ref, gbT_ref, gbC_ref, ofwd_ref, gate_ref, onw_ref,
         o_ref, s_ref, *scratch) = refs
    else:
        q_ref, k_ref, v_ref, kT_ref, gbT_ref, gbC_ref, o_ref, s_ref, *scratch = refs
    wq_ref, u_ref, qk_ref, kt_ref = scratch
    C = SCAN_CHUNK
    d = 1 if reverse else 0

    @pl.when(pl.program_id(1) == 0)
    def _():
        s_ref[...] = jnp.zeros_like(s_ref)

    ii = lax.broadcasted_iota(jnp.int32, (C, C), 0)
    jj = lax.broadcasted_iota(jnp.int32, (C, C), 1)
    incl = (ii <= jj) if reverse else (ii >= jj)
    strict = (ii < jj) if reverse else (ii > jj)
    last = 0 if reverse else C - 1

    n_chunks = tile // C
    order = list(range(n_chunks - 1, -1, -1) if reverse else range(n_chunks))
    items = [(c, h) for c in order for h in range(HEADS)]

    def views(c, h):
        rws = slice(c * C, (c + 1) * C)
        cols = slice(h * DK, (h + 1) * DK)
        ib = d * HEADS + h
        ig = 2 * HEADS + ib
        beta_c = gbC_ref[0, rws, ib:ib + 1]
        gc_c = gbC_ref[0, rws, ig:ig + 1]
        gc_r = gbT_ref[0, ig:ig + 1, rws]
        return rws, cols, beta_c, gc_c, gc_r

    qkks = []
    for c, h in items:
        rws, cols, _, _, _ = views(c, h)
        qkks.append(_bdot(jnp.concatenate([q_ref[0, rws, cols], k_ref[0, rws, cols]], axis=0),
                          kT_ref[0, cols, rws]))
    a_list = []
    for i, (c, h) in enumerate(items):
        _, _, beta_c, gc_c, gc_r = views(c, h)
        decay = jnp.where(incl, jnp.exp(jnp.where(incl, gc_c - gc_r, 0.0)), 0.0)
        qk_ref[i] = (qkks[i][:C] * decay).astype(BF16)
        a_list.append(jnp.where(strict, qkks[i][C:] * decay * beta_c, 0.0))
    ts = _unit_tri_inverses(a_list, ii, jj)
    sols = []
    for i, (c, h) in enumerate(items):
        rws, cols, beta_c, gc_c, _ = views(c, h)
        k = k_ref[0, rws, cols]
        rhs = jnp.concatenate([v_ref[0, rws, cols] * beta_c, k * (beta_c * jnp.exp(gc_c))], axis=1)
        sols.append(_bdot(ts[i], rhs))
    g_tots = []
    for i, (c, h) in enumerate(items):
        rws, cols, _, gc_c, gc_r = views(c, h)
        u_ref[i] = sols[i][:, :DK]
        wq_ref[i, 0:C, :] = sols[i][:, DK:].astype(BF16)
        wq_ref[i, C:2 * C, :] = (q_ref[0, rws, cols] * jnp.exp(gc_c)).astype(BF16)
        g_tot = gc_r[:, last:last + 1]
        kt_ref[i] = (kT_ref[0, cols, rws] * jnp.exp(g_tot - gc_r)).astype(BF16)
        g_tots.append(jnp.exp(g_tot))

    for ci, c in enumerate(order):
        idx = [ci * HEADS + h for h in range(HEADS)]
        ss = [s_ref[h] for h in range(HEADS)]
        wss = [_dot(wq_ref[i], s.astype(BF16)) for i, s in zip(idx, ss)]
        v_news = [(u_ref[i] - ws[:C]).astype(BF16) for i, ws in zip(idx, wss)]
        os_ = [ws[C:] + _dot(qk_ref[i], vn) for i, ws, vn in zip(idx, wss, v_news)]
        for h, (i, s, vn) in enumerate(zip(idx, ss, v_news)):
            s_ref[h] = s * g_tots[i] + _dot(kt_ref[i], vn)
        for h, o in enumerate(os_):
            rws, cols, _, _, _ = views(c, h)
            if final:
                o = o + ofwd_ref[0, rws, cols]
                o = o * lax.rsqrt(jnp.mean(o * o, -1, keepdims=True) + NORM_EPS) * onw_ref[...]
                o_ref[0, rws, cols] = (o * gate_ref[0, rws, cols]).astype(o_ref.dtype)
            else:
                o_ref[0, rws, cols] = o


def _delta_scan(q, k, v, kT, gbT, gbC, *, reverse, final_inputs=None):
    B, L, _ = q.shape
    tile = min(SEQ_TILE, L)
    n_tiles = L // tile
    n_items = (tile // SCAN_CHUNK) * HEADS
    final = final_inputs is not None
    tix = (lambda t: n_tiles - 1 - t) if reverse else (lambda t: t)
    seq_spec = lambda w: pl.BlockSpec((1, tile, w), lambda b, t: (b, tix(t), 0))
    in_specs = [seq_spec(DN_W), seq_spec(DN_W), seq_spec(DN_W),
                pl.BlockSpec((1, DN_W, tile), lambda b, t: (b, 0, tix(t))),
                pl.BlockSpec((1, 4 * HEADS, tile), lambda b, t: (b, 0, tix(t))),
                seq_spec(SCAN_CHUNK)]
    args = [q, k, v, kT, gbT, gbC]
    if final:
        ofwd, gate, onw = final_inputs
        in_specs += [seq_spec(DN_W), seq_spec(DN_W), _const_spec(onw.shape)]
        args += [ofwd, gate, onw]
    return pl.pallas_call(
        functools.partial(_delta_scan_kernel, tile=tile, reverse=reverse, final=final),
        grid=(B, n_tiles),
        in_specs=in_specs,
        out_specs=seq_spec(DN_W),
        out_shape=jax.ShapeDtypeStruct((B, L, DN_W), BF16 if final else F32),
        scratch_shapes=[pltpu.VMEM((HEADS, DK, DK), F32),
                        pltpu.VMEM((n_items, 2 * SCAN_CHUNK, DK), BF16),
                        pltpu.VMEM((n_items, SCAN_CHUNK, DK), F32),
                        pltpu.VMEM((n_items, SCAN_CHUNK, SCAN_CHUNK), BF16),
                        pltpu.VMEM((n_items, DK, SCAN_CHUNK), BF16)],
        compiler_params=pltpu.CompilerParams(dimension_semantics=("parallel", "arbitrary"),
                                             vmem_limit_bytes=VMEM_LIMIT),
        name="delta_scan_bwd" if reverse else "delta_scan_fwd",
    )(*args)


def _layer_params(l, ffn1_wg, ffn1_wu, ffn1_wd, ln1_g, ln1_b, w_in, conv_w, conv_b, conv_ln_g,
                  conv_ln_b, sconv_w, a_log, dt_bias, o_norm_w, w_out, ln2_g, ln2_b,
                  ffn2_wg, ffn2_wu, ffn2_wd, ln3_g, ln3_b):
    row = lambda a: a[l].reshape(1, -1).astype(F32)
    col = lambda a: a[l].reshape(-1, 1).astype(F32)
    wi = w_in[l].astype(BF16)
    o_q = 2 * CONV_W
    o_z = o_q + 3 * DN_W
    o_b = o_z + DN_W
    wo = w_out[l].astype(BF16)
    return dict(
        ffn1=(ffn1_wg[l].astype(BF16), ffn1_wu[l].astype(BF16), ffn1_wd[l].astype(BF16),
              row(ln1_g), row(ln1_b)),
        mixer_in=(wi[:, :o_q], wi[:, o_q:o_z], wi[:, o_z:o_b], wi[:, o_b:].T,
                  conv_w[l].astype(F32), row(conv_b), row(conv_ln_g), row(conv_ln_b),
                  sconv_w[l].astype(F32), col(a_log), col(dt_bias)),
        onw=row(o_norm_w),
        mixout=(wo[:CONV_W], wo[CONV_W:], row(ln2_g), row(ln2_b),
                ffn2_wg[l].astype(BF16), ffn2_wu[l].astype(BF16), ffn2_wd[l].astype(BF16),
                row(ln3_g), row(ln3_b)),
    )


def _trunk(x, layers):
    B, L, D = x.shape
    for p in layers:
        x1 = _ffn_ln(x.reshape(B * L, D), *p["ffn1"])
        cv, q, k, v, kT, gate, gbT, gbC = _mixer_in(x1.reshape(B, L, D), *p["mixer_in"])
        o_fwd = _delta_scan(q, k, v, kT, gbT, gbC, reverse=False)
        og = _delta_scan(q, k, v, kT, gbT, gbC, reverse=True, final_inputs=(o_fwd, gate, p["onw"]))
        x = _mixout_ffn(x1, cv.reshape(B * L, CONV_W), og.reshape(B * L, DN_W), *p["mixout"])
        x = x.reshape(B, L, D)
    return x


def kernel(x_prompt, x_sample, ffn1_wg, ffn1_wu, ffn1_wd, ln1_g, ln1_b, w_in, conv_w, conv_b, conv_ln_g, conv_ln_b, sconv_w, a_log, dt_bias, o_norm_w, w_out, ln2_g, ln2_b, ffn2_wg, ffn2_wu, ffn2_wd, ln3_g, ln3_b):
    weights = (ffn1_wg, ffn1_wu, ffn1_wd, ln1_g, ln1_b, w_in, conv_w, conv_b, conv_ln_g, conv_ln_b,
               sconv_w, a_log, dt_bias, o_norm_w, w_out, ln2_g, ln2_b, ffn2_wg, ffn2_wu, ffn2_wd,
               ln3_g, ln3_b)
    layers = [_layer_params(l, *weights) for l in range(DEPTH)]
    return (_trunk(x_prompt, layers), _trunk(x_sample, layers))
```

```python
import functools

import jax
import jax.numpy as jnp
from jax import lax
from jax.experimental import pallas as pl
from jax.experimental.pallas import tpu as pltpu

F32 = jnp.float32
BF16 = jnp.bfloat16

DEPTH = 2
D_MODEL = 1024
CONV_W = 512
DN_W = 512
HEADS = 4
DK = 128
CONV_K = 31
SC_K = 5
D_FF = 2816
ALPHA = (2 * DEPTH) ** 0.25
LN_EPS = 1e-5
NORM_EPS = 1e-6

SCAN_CHUNK = 128
SUBLANES = 8
HALO = 16
TOKEN_TILE = 512
SEQ_TILE = 512
SCAN_TILE = 512
MIX_UNIT = 256
CONV_ROWS = 32
VMEM_LIMIT = 56 * 1024 * 1024


def _dot(a, b):
    return jnp.dot(a, b, preferred_element_type=F32)


def _bdot(a, b):
    return jnp.dot(a.astype(BF16), b.astype(BF16), preferred_element_type=F32)


def _silu(x):
    return x * jax.nn.sigmoid(x)


def _layer_norm(r, g, b):
    mu = jnp.mean(r, -1, keepdims=True)
    c = r - mu
    var = jnp.mean(c * c, -1, keepdims=True)
    return c * lax.rsqrt(var + LN_EPS) * g + b


def _swiglu(x, wg_ref, wu_ref, wd_ref):
    xb = x.astype(BF16)
    hg = _dot(xb, wg_ref[...])
    hu = _dot(xb, wu_ref[...])
    a = (_silu(hg) * hu).astype(BF16)
    return _dot(a, wd_ref[...])


def _ffn_ln_kernel(x_ref, wg_ref, wu_ref, wd_ref, g_ref, b_ref, o_ref):
    x = x_ref[...]
    r = ALPHA * x + 0.5 * _swiglu(x, wg_ref, wu_ref, wd_ref)
    o_ref[...] = _layer_norm(r, g_ref[...], b_ref[...])


def _mixout_ffn_kernel(x_ref, cv_ref, og_ref, woc_ref, wod_ref, g2_ref, b2_ref,
                       wg_ref, wu_ref, wd_ref, g3_ref, b3_ref, o_ref):
    x = x_ref[...]
    m = _dot(cv_ref[...], woc_ref[...]) + _dot(og_ref[...], wod_ref[...])
    x2 = _layer_norm(ALPHA * x + m, g2_ref[...], b2_ref[...])
    r = ALPHA * x2 + 0.5 * _swiglu(x2, wg_ref, wu_ref, wd_ref)
    o_ref[...] = _layer_norm(r, g3_ref[...], b3_ref[...])


def _const_spec(shape):
    nd = len(shape)
    return pl.BlockSpec(shape, lambda *_: (0,) * nd, pipeline_mode=pl.Buffered(1))


def _row_spec(tm, width):
    return pl.BlockSpec((tm, width), lambda i: (i, 0))


def _ffn_ln(x, wg, wu, wd, g, b):
    n = x.shape[0]
    tm = min(TOKEN_TILE, n)
    return pl.pallas_call(
        _ffn_ln_kernel,
        grid=(n // tm,),
        in_specs=[_row_spec(tm, D_MODEL), _const_spec(wg.shape), _const_spec(wu.shape),
                  _const_spec(wd.shape), _const_spec(g.shape), _const_spec(b.shape)],
        out_specs=_row_spec(tm, D_MODEL),
        out_shape=jax.ShapeDtypeStruct((n, D_MODEL), F32),
        compiler_params=pltpu.CompilerParams(dimension_semantics=("parallel",),
                                             vmem_limit_bytes=VMEM_LIMIT),
        name="ffn_ln",
    )(x, wg, wu, wd, g, b)


def _mixout_ffn(x, cv, og, woc, wod, g2, b2, wg, wu, wd, g3, b3):
    n = x.shape[0]
    tm = min(TOKEN_TILE, n)
    consts = (woc, wod, g2, b2, wg, wu, wd, g3, b3)
    return pl.pallas_call(
        _mixout_ffn_kernel,
        grid=(n // tm,),
        in_specs=[_row_spec(tm, D_MODEL), _row_spec(tm, CONV_W), _row_spec(tm, DN_W)]
                 + [_const_spec(c.shape) for c in consts],
        out_specs=_row_spec(tm, D_MODEL),
        out_shape=jax.ShapeDtypeStruct((n, D_MODEL), F32),
        compiler_params=pltpu.CompilerParams(dimension_semantics=("parallel",),
                                             vmem_limit_bytes=VMEM_LIMIT),
        name="mixout_ffn",
    )(x, cv, og, *consts)


def _mixer_in_kernel(xp_ref, x_ref, xn_ref, perm_ref, permT_ref, wcvg_ref, wqkv_ref, wz_ref, wbaT_ref,
                     convw_ref, convb_ref, clng_ref, clnb_ref, sconvw_ref, alog_ref, dtb_ref,
                     cv_ref, q_ref, k_ref, v_ref, kT_ref, gate_ref, gbT_ref, gbC_ref,
                     xh_ref, ycat_ref, extc_ref, hc_ref, extq_ref, hq_ref, *, tile, n_tiles):
    t = pl.program_id(1)
    n_units = tile // MIX_UNIT
    P = MIX_UNIT // SUBLANES
    xb = x_ref[0].astype(BF16)
    for u in range(n_units):
        lo = u * MIX_UNIT
        if u == 0:
            left = jnp.where(t > 0, xp_ref[0], 0.0).astype(BF16)
        else:
            left = xb[lo - HALO:lo]
        if u == n_units - 1:
            right = jnp.where(t < n_tiles - 1, xn_ref[0], 0.0).astype(BF16)
        else:
            right = xb[lo + MIX_UNIT:lo + MIX_UNIT + HALO]
        xh_ref[u, 0:HALO, :] = left
        xh_ref[u, HALO:2 * HALO, :] = right
        xh_ref[u, 2 * HALO:, :] = _dot(perm_ref[...], xb[lo:lo + MIX_UNIT]).astype(BF16)

    def fill_extended(ext_ref, halo_ref, u, proj, reach):
        ext_ref[u, SUBLANES * reach:SUBLANES * reach + MIX_UNIT, :] = proj[2 * HALO:]
        halo_ref[u] = proj[0:2 * HALO]
        sub = lax.broadcasted_iota(jnp.int32, (SUBLANES, proj.shape[1]), 0)
        for m in range(-reach, 0):
            lo = SUBLANES * (m + P + reach)
            shifted = pltpu.roll(ext_ref[u, lo:lo + SUBLANES, :], 1, 0)
            edge = halo_ref[u, HALO + m:HALO + m + 1, :]
            ext_ref[u, SUBLANES * (m + reach):SUBLANES * (m + reach + 1), :] = jnp.where(sub == 0, edge, shifted)
        for m in range(P, P + reach):
            lo = SUBLANES * (m - P + reach)
            shifted = pltpu.roll(ext_ref[u, lo:lo + SUBLANES, :], SUBLANES - 1, 0)
            edge = halo_ref[u, HALO + m - P:HALO + m - P + 1, :]
            ext_ref[u, SUBLANES * (m + reach):SUBLANES * (m + reach + 1), :] = jnp.where(sub == SUBLANES - 1, edge, shifted)

    for u in range(n_units):
        xh = xh_ref[u]
        pc = _dot(xh, wcvg_ref[...])
        fill_extended(extc_ref, hc_ref, u, pc[:, :CONV_W] * jax.nn.sigmoid(pc[:, CONV_W:]), CONV_K // 2)
        fill_extended(extq_ref, hq_ref, u, _dot(xh, wqkv_ref[...]), SC_K // 2)

    convb = convb_ref[...]
    clng = clng_ref[...]
    clnb = clnb_ref[...]
    rows = 2 * CONV_ROWS
    for u in range(n_units):
        for r in range(MIX_UNIT // CONV_ROWS):
            acc = None
            for kk in range(CONV_K):
                lo = r * CONV_ROWS + SUBLANES * kk
                term = convw_ref[kk:kk + 1, :] * extc_ref[u, lo:lo + CONV_ROWS, :]
                acc = term if acc is None else acc + term
            c = _silu(_layer_norm(acc + convb, clng, clnb))
            ycat_ref[u, r * CONV_ROWS:(r + 1) * CONV_ROWS, 0:CONV_W] = c.astype(BF16)

        for j in range(3 * HEADS):
            cols = slice(j * DK, (j + 1) * DK)
            for r in range(MIX_UNIT // rows):
                acc = None
                for kk in range(SC_K):
                    lo = r * rows + SUBLANES * kk
                    term = sconvw_ref[kk:kk + 1, cols] * extq_ref[u, lo:lo + rows, cols]
                    acc = term if acc is None else acc + term
                y = _silu(acc)
                if j < HEADS:
                    y = y * lax.rsqrt(jnp.sum(y * y, -1, keepdims=True) + NORM_EPS) * (DK ** -0.5)
                elif j < 2 * HEADS:
                    y = y * lax.rsqrt(jnp.sum(y * y, -1, keepdims=True) + NORM_EPS)
                ycat_ref[u, r * rows:(r + 1) * rows, CONV_W + j * DK:CONV_W + (j + 1) * DK] = y.astype(BF16)

        ynat = _dot(permT_ref[...], ycat_ref[u])
        out_rows = slice(u * MIX_UNIT, (u + 1) * MIX_UNIT)
        cv_ref[0, out_rows, :] = ynat[:, 0:CONV_W].astype(BF16)
        q_ref[0, out_rows, :] = ynat[:, CONV_W:CONV_W + DN_W].astype(BF16)
        k_ref[0, out_rows, :] = ynat[:, CONV_W + DN_W:CONV_W + 2 * DN_W].astype(BF16)
        v_ref[0, out_rows, :] = ynat[:, CONV_W + 2 * DN_W:].astype(BF16)
        for h in range(HEADS):
            kcol = CONV_W + DN_W + h * DK
            for c in range(MIX_UNIT // SCAN_CHUNK):
                rws = slice(c * SCAN_CHUNK, (c + 1) * SCAN_CHUNK)
                orws = slice(u * MIX_UNIT + c * SCAN_CHUNK, u * MIX_UNIT + (c + 1) * SCAN_CHUNK)
                kT_ref[0, h * DK:(h + 1) * DK, orws] = ynat[rws, kcol:kcol + DK].T.astype(BF16)
    n_chunks = tile // SCAN_CHUNK

    xm = xb
    gate_ref[0] = _silu(_dot(xm, wz_ref[...]))

    pb = lax.dot_general(wbaT_ref[...], xm, (((1,), (1,)), ((), ())),
                         preferred_element_type=F32)
    beta = jax.nn.sigmoid(pb[0:2 * HEADS, :])
    a = pb[2 * HEADS:, :] + dtb_ref[...]
    softplus = jnp.maximum(a, 0.0) + jnp.log1p(jnp.exp(-jnp.abs(a)))
    g = -jnp.exp(alog_ref[...]) * softplus
    lane = lax.broadcasted_iota(jnp.int32, (2 * HEADS, SCAN_CHUNK), 1)
    fwd_row = lax.broadcasted_iota(jnp.int32, (2 * HEADS, SCAN_CHUNK), 0) < HEADS
    zpad = jnp.zeros((SCAN_CHUNK - 4 * HEADS, SCAN_CHUNK), F32)
    for c in range(n_chunks):
        cl = slice(c * SCAN_CHUNK, (c + 1) * SCAN_CHUNK)
        x = g[:, cl]
        s = 1
        while s < SCAN_CHUNK:
            pre = jnp.where(lane >= s, pltpu.roll(x, s, 1), 0.0)
            suf = jnp.where(lane < SCAN_CHUNK - s, pltpu.roll(x, SCAN_CHUNK - s, 1), 0.0)
            x = x + jnp.where(fwd_row, pre, suf)
            s *= 2
        rows16 = jnp.concatenate([beta[:, cl], x], axis=0)
        gbT_ref[0, :, cl] = rows16
        gbC_ref[0, cl, :] = jnp.concatenate([rows16, zpad], axis=0).T


def _mixer_in(x, wcvg, wqkv, wz, wbaT, convw, convb, clng, clnb, sconvw, alog, dtb):
    B, L, D = x.shape
    tile = min(SEQ_TILE, L)
    n_tiles = L // tile
    hb = tile // HALO
    n_hb = L // HALO
    unit = MIX_UNIT
    n_units = tile // unit
    r = jnp.arange(unit)
    perm = (r[None, :] == ((r // SUBLANES) + (unit // SUBLANES) * (r % SUBLANES))[:, None]).astype(BF16)
    consts = (perm, perm.T, wcvg, wqkv, wz, wbaT, convw, convb, clng, clnb, sconvw, alog, dtb)
    seq = lambda w, dt: jax.ShapeDtypeStruct((B, L, w), dt)
    seq_spec = lambda w: pl.BlockSpec((1, tile, w), lambda b, t: (b, t, 0))
    return pl.pallas_call(
        functools.partial(_mixer_in_kernel, tile=tile, n_tiles=n_tiles),
        grid=(B, n_tiles),
        in_specs=[pl.BlockSpec((1, HALO, D), lambda b, t: (b, jnp.maximum(t * hb - 1, 0), 0)),
                  pl.BlockSpec((1, tile, D), lambda b, t: (b, t, 0)),
                  pl.BlockSpec((1, HALO, D), lambda b, t: (b, jnp.minimum((t + 1) * hb, n_hb - 1), 0))]
                 + [_const_spec(c.shape) for c in consts],
        out_specs=[seq_spec(CONV_W), seq_spec(DN_W), seq_spec(DN_W), seq_spec(DN_W),
                   pl.BlockSpec((1, DN_W, tile), lambda b, t: (b, 0, t)),
                   seq_spec(DN_W),
                   pl.BlockSpec((1, 4 * HEADS, tile), lambda b, t: (b, 0, t)),
                   seq_spec(SCAN_CHUNK)],
        out_shape=[seq(CONV_W, BF16), seq(DN_W, BF16), seq(DN_W, BF16), seq(DN_W, BF16),
                   jax.ShapeDtypeStruct((B, DN_W, L), BF16),
                   seq(DN_W, F32),
                   jax.ShapeDtypeStruct((B, 4 * HEADS, L), F32),
                   seq(SCAN_CHUNK, F32)],
        scratch_shapes=[pltpu.VMEM((n_units, unit + 2 * HALO, D), BF16),
                        pltpu.VMEM((n_units, unit, CONV_W + 3 * DN_W), BF16),
                        pltpu.VMEM((n_units, unit + 2 * SUBLANES * (CONV_K // 2), CONV_W), F32),
                        pltpu.VMEM((n_units, 2 * HALO, CONV_W), F32),
                        pltpu.VMEM((n_units, unit + 2 * SUBLANES * (SC_K // 2), 3 * DN_W), F32),
                        pltpu.VMEM((n_units, 2 * HALO, 3 * DN_W), F32)],
        compiler_params=pltpu.CompilerParams(dimension_semantics=("parallel", "parallel"),
                                             vmem_limit_bytes=VMEM_LIMIT),
        name="mixer_in",
    )(x, x, x, *consts)


def _unit_tri_inverses(a_list, ii, jj, reverse):
    eye = (ii == jj).astype(F32)
    blk = (ii >> 3) == (jj >> 3)
    ns = [jnp.where(blk, -a, 0.0) for a in a_list]
    ts = [eye + n for n in ns]
    ps = _level_dots(ns, ns)
    ts = [t + pt for t, pt in zip(ts, _level_dots(ps, ts))]
    ps = _level_dots(ps, ps)
    ts = [t + pt for t, pt in zip(ts, _level_dots(ps, ts))]
    sh = 3
    while (1 << sh) < SCAN_CHUNK:
        bs = 1 << sh
        starts = [(2 * m + (0 if reverse else 1)) * bs for m in range(SCAN_CHUNK // (2 * bs))]
        take = lambda x: jnp.concatenate([x[s:s + bs] for s in starts], axis=0)
        zero = jnp.zeros((bs, SCAN_CHUNK), F32)

        def spread(xr):
            parts = []
            for i in range(len(starts)):
                blk_rows = xr[i * bs:(i + 1) * bs]
                parts += [blk_rows, zero] if reverse else [zero, blk_rows]
            return jnp.concatenate(parts, axis=0)

        def put(x, xr):
            parts = []
            for i in range(len(starts)):
                new = xr[i * bs:(i + 1) * bs]
                keep = x[(2 * i + (1 if reverse else 0)) * bs:(2 * i + (2 if reverse else 1)) * bs]
                parts += [new, keep] if reverse else [keep, new]
            return jnp.concatenate(parts, axis=0)

        m = ((ii >> (sh + 1)) == (jj >> (sh + 1))) & ((ii >> sh) != (jj >> sh))
        ets = _level_dots([take(jnp.where(m, a, 0.0)) for a in a_list], ts)
        trs = [take(t) for t in ts]
        tets = _level_dots(trs, [spread(et) for et in ets])
        ts = [put(t, tr - tet) for t, tr, tet in zip(ts, trs, tets)]
        sh += 1
    return ts


def _level_dots(xs, ys):
    return [_bdot(x, y) for x, y in zip(xs, ys)]


def _delta_scan_kernel(*refs, tile, reverse, final):
    if final:
        (q_ref, k_ref, v_ref, kT_ref, gbT_ref, gbC_ref, ofwd_ref, gate_ref, onw_ref,
         o_ref, s_ref, *scratch) = refs
    else:
        q_ref, k_ref, v_ref, kT_ref, gbT_ref, gbC_ref, o_ref, s_ref, *scratch = refs
    qk_ref, n_ref, m_ref, o0_ref, qp_ref = scratch
    C = SCAN_CHUNK
    d = 1 if reverse else 0

    @pl.when(pl.program_id(1) == 0)
    def _():
        s_ref[...] = jnp.zeros_like(s_ref)

    ii = lax.broadcasted_iota(jnp.int32, (C, C), 0)
    jj = lax.broadcasted_iota(jnp.int32, (C, C), 1)
    incl = (ii <= jj) if reverse else (ii >= jj)
    strict = (ii < jj) if reverse else (ii > jj)
    last = 0 if reverse else C - 1

    n_chunks = tile // C
    order = list(range(n_chunks - 1, -1, -1) if reverse else range(n_chunks))
    items = [(c, h) for c in order for h in range(HEADS)]

    def views(c, h):
        rws = slice(c * C, (c + 1) * C)
        cols = slice(h * DK, (h + 1) * DK)
        ib = d * HEADS + h
        ig = 2 * HEADS + ib
        beta_c = gbC_ref[0, rws, ib:ib + 1]
        gc_c = gbC_ref[0, rws, ig:ig + 1]
        gc_r = gbT_ref[0, ig:ig + 1, rws]
        return rws, cols, beta_c, gc_c, gc_r

    qks, kts = [], []
    for c, h in items:
        rws, cols, _, _, _ = views(c, h)
        qks.append(jnp.concatenate([q_ref[0, rws, cols], k_ref[0, rws, cols]], axis=0))
        kts.append(kT_ref[0, cols, rws])
    qkks = _level_dots(qks, kts)
    a_list = []
    for i, (c, h) in enumerate(items):
        _, _, beta_c, gc_c, gc_r = views(c, h)
        decay = jnp.where(incl, jnp.exp(jnp.where(incl, gc_c - gc_r, 0.0)), 0.0)
        qk_ref[i] = (qkks[i][:C] * decay).astype(BF16)
        a_list.append(jnp.where(strict, qkks[i][C:] * decay * beta_c, 0.0))
    ts = _unit_tri_inverses(a_list, ii, jj, reverse)
    sols = []
    for i, (c, h) in enumerate(items):
        rws, cols, beta_c, gc_c, _ = views(c, h)
        k = k_ref[0, rws, cols]
        rhs = jnp.concatenate([v_ref[0, rws, cols] * beta_c, k * (beta_c * jnp.exp(gc_c))], axis=1)
        sols.append(_bdot(ts[i], rhs))
    g_tots, kts_tail = [], []
    for i, (c, h) in enumerate(items):
        rws, cols, _, gc_c, gc_r = views(c, h)
        g_tot = gc_r[:, last:last + 1]
        kts_tail.append(kT_ref[0, cols, rws] * jnp.exp(g_tot - gc_r))
        g_tots.append(jnp.exp(g_tot))
    sols_b = [sol.astype(BF16) for sol in sols]
    kt_uw = _level_dots(kts_tail, sols_b)
    qk_uw = _level_dots([qk_ref[i] for i in range(len(items))], sols_b)
    for i, (c, h) in enumerate(items):
        rws, cols, _, gc_c, _ = views(c, h)
        n_ref[i] = kt_uw[i][:, :DK]
        m_ref[i] = kt_uw[i][:, DK:].astype(BF16)
        o0_ref[i] = qk_uw[i][:, :DK]
        qp_ref[i] = (q_ref[0, rws, cols] * jnp.exp(gc_c) - qk_uw[i][:, DK:]).astype(BF16)

    for ci, c in enumerate(order):
        idx = [ci * HEADS + h for h in range(HEADS)]
        ss = [s_ref[h] for h in range(HEADS)]
        ss_b = [s.astype(BF16) for s in ss]
        mss = _level_dots([m_ref[i] for i in idx], ss_b)
        for h, (i, s, ms) in enumerate(zip(idx, ss, mss)):
            s_ref[h] = s * g_tots[i] + n_ref[i] - ms
        qss = _level_dots([qp_ref[i] for i in idx], ss_b)
        os_ = [o0_ref[i] + qs for i, qs in zip(idx, qss)]
        for h, o in enumerate(os_):
            rws, cols, _, _, _ = views(c, h)
            if final:
                o = o + ofwd_ref[0, rws, cols]
                o = o * lax.rsqrt(jnp.mean(o * o, -1, keepdims=True) + NORM_EPS) * onw_ref[...]
                o_ref[0, rws, cols] = (o * gate_ref[0, rws, cols]).astype(o_ref.dtype)
            else:
                o_ref[0, rws, cols] = o


def _delta_scan(q, k, v, kT, gbT, gbC, *, reverse, final_inputs=None):
    B, L, _ = q.shape
    tile = min(SCAN_TILE, L)
    n_tiles = L // tile
    n_items = (tile // SCAN_CHUNK) * HEADS
    final = final_inputs is not None
    tix = (lambda t: n_tiles - 1 - t) if reverse else (lambda t: t)
    seq_spec = lambda w: pl.BlockSpec((1, tile, w), lambda b, t: (b, tix(t), 0))
    in_specs = [seq_spec(DN_W), seq_spec(DN_W), seq_spec(DN_W),
                pl.BlockSpec((1, DN_W, tile), lambda b, t: (b, 0, tix(t))),
                pl.BlockSpec((1, 4 * HEADS, tile), lambda b, t: (b, 0, tix(t))),
                seq_spec(SCAN_CHUNK)]
    args = [q, k, v, kT, gbT, gbC]
    if final:
        ofwd, gate, onw = final_inputs
        in_specs += [seq_spec(DN_W), seq_spec(DN_W), _const_spec(onw.shape)]
        args += [ofwd, gate, onw]
    return pl.pallas_call(
        functools.partial(_delta_scan_kernel, tile=tile, reverse=reverse, final=final),
        grid=(B, n_tiles),
        in_specs=in_specs,
        out_specs=seq_spec(DN_W),
        out_shape=jax.ShapeDtypeStruct((B, L, DN_W), BF16 if final else F32),
        scratch_shapes=[pltpu.VMEM((HEADS, DK, DK), F32),
                        pltpu.VMEM((n_items, SCAN_CHUNK, SCAN_CHUNK), BF16),
                        pltpu.VMEM((n_items, DK, DK), F32),
                        pltpu.VMEM((n_items, DK, DK), BF16),
                        pltpu.VMEM((n_items, SCAN_CHUNK, DK), F32),
                        pltpu.VMEM((n_items, SCAN_CHUNK, DK), BF16)],
        compiler_params=pltpu.CompilerParams(dimension_semantics=("parallel", "arbitrary"),
                                             vmem_limit_bytes=VMEM_LIMIT),
        name="delta_scan_bwd" if reverse else "delta_scan_fwd",
    )(*args)


def _layer_params(l, ffn1_wg, ffn1_wu, ffn1_wd, ln1_g, ln1_b, w_in, conv_w, conv_b, conv_ln_g,
                  conv_ln_b, sconv_w, a_log, dt_bias, o_norm_w, w_out, ln2_g, ln2_b,
                  ffn2_wg, ffn2_wu, ffn2_wd, ln3_g, ln3_b):
    row = lambda a: a[l].reshape(1, -1).astype(F32)
    col = lambda a: a[l].reshape(-1, 1).astype(F32)
    wi = w_in[l].astype(BF16)
    o_q = 2 * CONV_W
    o_z = o_q + 3 * DN_W
    o_b = o_z + DN_W
    wo = w_out[l].astype(BF16)
    return dict(
        ffn1=(ffn1_wg[l].astype(BF16), ffn1_wu[l].astype(BF16), ffn1_wd[l].astype(BF16),
              row(ln1_g), row(ln1_b)),
        mixer_in=(wi[:, :o_q], wi[:, o_q:o_z], wi[:, o_z:o_b], wi[:, o_b:].T,
                  conv_w[l].astype(F32), row(conv_b), row(conv_ln_g), row(conv_ln_b),
                  sconv_w[l].astype(F32), col(a_log), col(dt_bias)),
        onw=row(o_norm_w),
        mixout=(wo[:CONV_W], wo[CONV_W:], row(ln2_g), row(ln2_b),
                ffn2_wg[l].astype(BF16), ffn2_wu[l].astype(BF16), ffn2_wd[l].astype(BF16),
                row(ln3_g), row(ln3_b)),
    )


def _trunk(x, layers):
    B, L, D = x.shape
    for p in layers:
        x1 = _ffn_ln(x.reshape(B * L, D), *p["ffn1"])
        cv, q, k, v, kT, gate, gbT, gbC = _mixer_in(x1.reshape(B, L, D), *p["mixer_in"])
        o_fwd = _delta_scan(q, k, v, kT, gbT, gbC, reverse=False)
        og = _delta_scan(q, k, v, kT, gbT, gbC, reverse=True, final_inputs=(o_fwd, gate, p["onw"]))
        x = _mixout_ffn(x1, cv.reshape(B * L, CONV_W), og.reshape(B * L, DN_W), *p["mixout"])
        x = x.reshape(B, L, D)
    return x


def kernel(x_prompt, x_sample, ffn1_wg, ffn1_wu, ffn1_wd, ln1_g, ln1_b, w_in, conv_w, conv_b, conv_ln_g, conv_ln_b, sconv_w, a_log, dt_bias, o_norm_w, w_out, ln2_g, ln2_b, ffn2_wg, ffn2_wu, ffn2_wd, ln3_g, ln3_b):
    weights = (ffn1_wg, ffn1_wu, ffn1_wd, ln1_g, ln1_b, w_in, conv_w, conv_b, conv_ln_g, conv_ln_b,
               sconv_w, a_log, dt_bias, o_norm_w, w_out, ln2_g, ln2_b, ffn2_wg, ffn2_wu, ffn2_wd,
               ln3_g, ln3_b)
    layers = [_layer_params(l, *weights) for l in range(DEPTH)]
    return (_trunk(x_prompt, layers), _trunk(x_sample, layers))
```

```python
import functools

import jax
import jax.numpy as jnp
from jax import lax
from jax.experimental import pallas as pl
from jax.experimental.pallas import tpu as pltpu

F32 = jnp.float32
BF16 = jnp.bfloat16

DEPTH = 2
D_MODEL = 1024
CONV_W = 512
DN_W = 512
HEADS = 4
DK = 128
CONV_K = 31
SC_K = 5
D_FF = 2816
ALPHA = (2 * DEPTH) ** 0.25
LN_EPS = 1e-5
NORM_EPS = 1e-6

SCAN_CHUNK = 128
SUBLANES = 8
HALO = 16
TOKEN_TILE = 512
SEQ_TILE = 1024
SCAN_TILE = 512
MIX_UNIT = 256
PHASE1_OFFSET = 3
CONV_ROWS = 32
VMEM_LIMIT = 56 * 1024 * 1024


def _dot(a, b):
    return jnp.dot(a, b, preferred_element_type=F32)


def _bdot(a, b):
    return jnp.dot(a.astype(BF16), b.astype(BF16), preferred_element_type=F32)


def _silu(x):
    return x * jax.nn.sigmoid(x)


def _layer_norm(r, g, b):
    mu = jnp.mean(r, -1, keepdims=True)
    c = r - mu
    var = jnp.mean(c * c, -1, keepdims=True)
    return c * lax.rsqrt(var + LN_EPS) * g + b


def _swiglu(x, wg_ref, wu_ref, wd_ref):
    xb = x.astype(BF16)
    hg = _dot(xb, wg_ref[...])
    hu = _dot(xb, wu_ref[...])
    a = (_silu(hg) * hu).astype(BF16)
    return _dot(a, wd_ref[...])


def _ffn_ln_kernel(x_ref, wg_ref, wu_ref, wd_ref, g_ref, b_ref, o_ref):
    x = x_ref[...]
    r = ALPHA * x + 0.5 * _swiglu(x, wg_ref, wu_ref, wd_ref)
    o_ref[...] = _layer_norm(r, g_ref[...], b_ref[...])


def _mixout_ffn_kernel(x_ref, cv_ref, og_ref, woc_ref, wod_ref, g2_ref, b2_ref,
                       wg_ref, wu_ref, wd_ref, g3_ref, b3_ref, o_ref):
    x = x_ref[...]
    m = _dot(cv_ref[...], woc_ref[...]) + _dot(og_ref[...], wod_ref[...])
    x2 = _layer_norm(ALPHA * x + m, g2_ref[...], b2_ref[...])
    r = ALPHA * x2 + 0.5 * _swiglu(x2, wg_ref, wu_ref, wd_ref)
    o_ref[...] = _layer_norm(r, g3_ref[...], b3_ref[...])


def _const_spec(shape):
    nd = len(shape)
    return pl.BlockSpec(shape, lambda *_: (0,) * nd, pipeline_mode=pl.Buffered(1))


def _row_spec(tm, width):
    return pl.BlockSpec((tm, width), lambda i: (i, 0))


def _ffn_ln(x, wg, wu, wd, g, b):
    n = x.shape[0]
    tm = min(TOKEN_TILE, n)
    return pl.pallas_call(
        _ffn_ln_kernel,
        grid=(n // tm,),
        in_specs=[_row_spec(tm, D_MODEL), _const_spec(wg.shape), _const_spec(wu.shape),
                  _const_spec(wd.shape), _const_spec(g.shape), _const_spec(b.shape)],
        out_specs=_row_spec(tm, D_MODEL),
        out_shape=jax.ShapeDtypeStruct((n, D_MODEL), F32),
        compiler_params=pltpu.CompilerParams(dimension_semantics=("parallel",),
                                             vmem_limit_bytes=VMEM_LIMIT),
        name="ffn_ln",
    )(x, wg, wu, wd, g, b)


def _mixout_ffn(x, cv, og, woc, wod, g2, b2, wg, wu, wd, g3, b3):
    n = x.shape[0]
    tm = min(TOKEN_TILE, n)
    consts = (woc, wod, g2, b2, wg, wu, wd, g3, b3)
    return pl.pallas_call(
        _mixout_ffn_kernel,
        grid=(n // tm,),
        in_specs=[_row_spec(tm, D_MODEL), _row_spec(tm, CONV_W), _row_spec(tm, DN_W)]
                 + [_const_spec(c.shape) for c in consts],
        out_specs=_row_spec(tm, D_MODEL),
        out_shape=jax.ShapeDtypeStruct((n, D_MODEL), F32),
        compiler_params=pltpu.CompilerParams(dimension_semantics=("parallel",),
                                             vmem_limit_bytes=VMEM_LIMIT),
        name="mixout_ffn",
    )(x, cv, og, *consts)


def _mixer_in_kernel(xp_ref, x_ref, xn_ref, perm_ref, permT_ref, wcvg_ref, wqkv_ref, wz_ref, wbaT_ref,
                     convw_ref, convb_ref, clng_ref, clnb_ref, sconvw_ref, alog_ref, dtb_ref,
                     cv_ref, q_ref, k_ref, v_ref, kT_ref, gate_ref, gbT_ref, gbC_ref,
                     *scratch, tile, n_tiles):
    t = pl.program_id(1)
    n_units = tile // MIX_UNIT
    xh_refs, ycat_refs, extc_refs, hc_refs, extq_refs, hq_refs = (
        scratch[i * n_units:(i + 1) * n_units] for i in range(6))
    P = MIX_UNIT // SUBLANES
    xb = x_ref[0].astype(BF16)
    for u in range(n_units):
        lo = u * MIX_UNIT
        if u == 0:
            left = jnp.where(t > 0, xp_ref[0], 0.0).astype(BF16)
        else:
            left = xb[lo - HALO:lo]
        if u == n_units - 1:
            right = jnp.where(t < n_tiles - 1, xn_ref[0], 0.0).astype(BF16)
        else:
            right = xb[lo + MIX_UNIT:lo + MIX_UNIT + HALO]
        xh_refs[u][0:HALO, :] = left
        xh_refs[u][HALO:2 * HALO, :] = right
        xh_refs[u][2 * HALO:, :] = _dot(perm_ref[...], xb[lo:lo + MIX_UNIT]).astype(BF16)

    def fill_extended(ext_ref, halo_ref, proj, reach):
        ext_ref[SUBLANES * reach:SUBLANES * reach + MIX_UNIT, :] = proj[2 * HALO:]
        halo_ref[...] = proj[0:2 * HALO]
        sub = lax.broadcasted_iota(jnp.int32, (SUBLANES, proj.shape[1]), 0)
        for m in range(-reach, 0):
            lo = SUBLANES * (m + P + reach)
            shifted = pltpu.roll(ext_ref[lo:lo + SUBLANES, :], 1, 0)
            edge = halo_ref[HALO + m:HALO + m + 1, :]
            ext_ref[SUBLANES * (m + reach):SUBLANES * (m + reach + 1), :] = jnp.where(sub == 0, edge, shifted)
        for m in range(P, P + reach):
            lo = SUBLANES * (m - P + reach)
            shifted = pltpu.roll(ext_ref[lo:lo + SUBLANES, :], SUBLANES - 1, 0)
            edge = halo_ref[HALO + m - P:HALO + m - P + 1, :]
            ext_ref[SUBLANES * (m + reach):SUBLANES * (m + reach + 1), :] = jnp.where(sub == SUBLANES - 1, edge, shifted)

    for u in range(n_units):
        xh = xh_refs[u][...]
        pc = _dot(xh, wcvg_ref[...])
        fill_extended(extc_refs[u], hc_refs[u], pc[:, :CONV_W] * jax.nn.sigmoid(pc[:, CONV_W:]), CONV_K // 2)
        fill_extended(extq_refs[u], hq_refs[u], _dot(xh, wqkv_ref[...]), SC_K // 2)

    convb = convb_ref[...]
    clng = clng_ref[...]
    clnb = clnb_ref[...]
    rows = 2 * CONV_ROWS
    for u in range(n_units):
        for r in range(MIX_UNIT // CONV_ROWS):
            acc = None
            for kk in range(CONV_K):
                lo = r * CONV_ROWS + SUBLANES * kk
                term = convw_ref[kk:kk + 1, :] * extc_refs[u][lo:lo + CONV_ROWS, :]
                acc = term if acc is None else acc + term
            c = _silu(_layer_norm(acc + convb, clng, clnb))
            ycat_refs[u][r * CONV_ROWS:(r + 1) * CONV_ROWS, 0:CONV_W] = c.astype(BF16)

        for j in range(3 * HEADS):
            cols = slice(j * DK, (j + 1) * DK)
            for r in range(MIX_UNIT // rows):
                acc = None
                for kk in range(SC_K):
                    lo = r * rows + SUBLANES * kk
                    term = sconvw_ref[kk:kk + 1, cols] * extq_refs[u][lo:lo + rows, cols]
                    acc = term if acc is None else acc + term
                y = _silu(acc)
                if j < HEADS:
                    y = y * lax.rsqrt(jnp.sum(y * y, -1, keepdims=True) + NORM_EPS) * (DK ** -0.5)
                elif j < 2 * HEADS:
                    y = y * lax.rsqrt(jnp.sum(y * y, -1, keepdims=True) + NORM_EPS)
                ycat_refs[u][r * rows:(r + 1) * rows, CONV_W + j * DK:CONV_W + (j + 1) * DK] = y.astype(BF16)

        ynat = _dot(permT_ref[...], ycat_refs[u][...])
        out_rows = slice(u * MIX_UNIT, (u + 1) * MIX_UNIT)
        cv_ref[0, out_rows, :] = ynat[:, 0:CONV_W].astype(BF16)
        q_ref[0, out_rows, :] = ynat[:, CONV_W:CONV_W + DN_W].astype(BF16)
        k_ref[0, out_rows, :] = ynat[:, CONV_W + DN_W:CONV_W + 2 * DN_W].astype(BF16)
        v_ref[0, out_rows, :] = ynat[:, CONV_W + 2 * DN_W:].astype(BF16)
        for h in range(HEADS):
            kcol = CONV_W + DN_W + h * DK
            for c in range(MIX_UNIT // SCAN_CHUNK):
                rws = slice(c * SCAN_CHUNK, (c + 1) * SCAN_CHUNK)
                orws = slice(u * MIX_UNIT + c * SCAN_CHUNK, u * MIX_UNIT + (c + 1) * SCAN_CHUNK)
                kT_ref[0, h * DK:(h + 1) * DK, orws] = ynat[rws, kcol:kcol + DK].T.astype(BF16)
    n_chunks = tile // SCAN_CHUNK

    xm = xb
    gate_ref[0] = _silu(_dot(xm, wz_ref[...]))

    pb = lax.dot_general(wbaT_ref[...], xm, (((1,), (1,)), ((), ())),
                         preferred_element_type=F32)
    beta = jax.nn.sigmoid(pb[0:2 * HEADS, :])
    a = pb[2 * HEADS:, :] + dtb_ref[...]
    softplus = jnp.maximum(a, 0.0) + jnp.log1p(jnp.exp(-jnp.abs(a)))
    g = -jnp.exp(alog_ref[...]) * softplus
    lane = lax.broadcasted_iota(jnp.int32, (2 * HEADS, SCAN_CHUNK), 1)
    fwd_row = lax.broadcasted_iota(jnp.int32, (2 * HEADS, SCAN_CHUNK), 0) < HEADS
    zpad = jnp.zeros((SCAN_CHUNK - 4 * HEADS, SCAN_CHUNK), F32)
    for c in range(n_chunks):
        cl = slice(c * SCAN_CHUNK, (c + 1) * SCAN_CHUNK)
        x = g[:, cl]
        s = 1
        while s < SCAN_CHUNK:
            pre = jnp.where(lane >= s, pltpu.roll(x, s, 1), 0.0)
            suf = jnp.where(lane < SCAN_CHUNK - s, pltpu.roll(x, SCAN_CHUNK - s, 1), 0.0)
            x = x + jnp.where(fwd_row, pre, suf)
            s *= 2
        rows16 = jnp.concatenate([beta[:, cl], x], axis=0)
        gbT_ref[0, :, cl] = rows16
        gbC_ref[0, cl, :] = jnp.concatenate([rows16, zpad], axis=0).T


def _mixer_in(x, wcvg, wqkv, wz, wbaT, convw, convb, clng, clnb, sconvw, alog, dtb):
    B, L, D = x.shape
    tile = min(SEQ_TILE, L)
    n_tiles = L // tile
    hb = tile // HALO
    n_hb = L // HALO
    unit = MIX_UNIT
    n_units = tile // unit
    r = jnp.arange(unit)
    perm = (r[None, :] == ((r // SUBLANES) + (unit // SUBLANES) * (r % SUBLANES))[:, None]).astype(BF16)
    consts = (perm, perm.T, wcvg, wqkv, wz, wbaT, convw, convb, clng, clnb, sconvw, alog, dtb)
    seq = lambda w, dt: jax.ShapeDtypeStruct((B, L, w), dt)
    seq_spec = lambda w: pl.BlockSpec((1, tile, w), lambda b, t: (b, t, 0))
    return pl.pallas_call(
        functools.partial(_mixer_in_kernel, tile=tile, n_tiles=n_tiles),
        grid=(B, n_tiles),
        in_specs=[pl.BlockSpec((1, HALO, D), lambda b, t: (b, jnp.maximum(t * hb - 1, 0), 0)),
                  pl.BlockSpec((1, tile, D), lambda b, t: (b, t, 0)),
                  pl.BlockSpec((1, HALO, D), lambda b, t: (b, jnp.minimum((t + 1) * hb, n_hb - 1), 0))]
                 + [_const_spec(c.shape) for c in consts],
        out_specs=[seq_spec(CONV_W), seq_spec(DN_W), seq_spec(DN_W), seq_spec(DN_W),
                   pl.BlockSpec((1, DN_W, tile), lambda b, t: (b, 0, t)),
                   seq_spec(DN_W),
                   pl.BlockSpec((1, 4 * HEADS, tile), lambda b, t: (b, 0, t)),
                   seq_spec(SCAN_CHUNK)],
        out_shape=[seq(CONV_W, BF16), seq(DN_W, BF16), seq(DN_W, BF16), seq(DN_W, BF16),
                   jax.ShapeDtypeStruct((B, DN_W, L), BF16),
                   seq(DN_W, F32),
                   jax.ShapeDtypeStruct((B, 4 * HEADS, L), F32),
                   seq(SCAN_CHUNK, F32)],
        scratch_shapes=[pltpu.VMEM(shape, dt) for shape, dt in (
            ((unit + 2 * HALO, D), BF16),
            ((unit, CONV_W + 3 * DN_W), BF16),
            ((unit + 2 * SUBLANES * (CONV_K // 2), CONV_W), F32),
            ((2 * HALO, CONV_W), F32),
            ((unit + 2 * SUBLANES * (SC_K // 2), 3 * DN_W), F32),
            ((2 * HALO, 3 * DN_W), F32)) for _ in range(n_units)],
        compiler_params=pltpu.CompilerParams(dimension_semantics=("parallel", "parallel"),
                                             vmem_limit_bytes=VMEM_LIMIT),
        name="mixer_in",
    )(x, x, x, *consts)


def _unit_tri_inverse_steps(a_list, ii, jj, reverse):
    eye = (ii == jj).astype(F32)
    blk = (ii >> 3) == (jj >> 3)
    ns = [jnp.where(blk, -a, 0.0) for a in a_list]
    ts = [eye + n for n in ns]
    ps = _level_dots(ns, ns)
    yield
    ts = [t + pt for t, pt in zip(ts, _level_dots(ps, ts))]
    yield
    ps = _level_dots(ps, ps)
    yield
    ts = [t + pt for t, pt in zip(ts, _level_dots(ps, ts))]
    yield
    sh = 3
    while (1 << sh) < SCAN_CHUNK:
        bs = 1 << sh
        starts = [(2 * m + (0 if reverse else 1)) * bs for m in range(SCAN_CHUNK // (2 * bs))]
        take = lambda x: jnp.concatenate([x[s:s + bs] for s in starts], axis=0)
        zero = jnp.zeros((bs, SCAN_CHUNK), F32)

        def spread(xr):
            parts = []
            for i in range(len(starts)):
                blk_rows = xr[i * bs:(i + 1) * bs]
                parts += [blk_rows, zero] if reverse else [zero, blk_rows]
            return jnp.concatenate(parts, axis=0)

        def put(x, xr):
            parts = []
            for i in range(len(starts)):
                new = xr[i * bs:(i + 1) * bs]
                keep = x[(2 * i + (1 if reverse else 0)) * bs:(2 * i + (2 if reverse else 1)) * bs]
                parts += [new, keep] if reverse else [keep, new]
            return jnp.concatenate(parts, axis=0)

        m = ((ii >> (sh + 1)) == (jj >> (sh + 1))) & ((ii >> sh) != (jj >> sh))
        ets = _level_dots([take(jnp.where(m, a, 0.0)) for a in a_list], ts)
        yield
        trs = [take(t) for t in ts]
        tets = _level_dots(trs, [spread(et) for et in ets])
        ts = [put(t, tr - tet) for t, tr, tet in zip(ts, trs, tets)]
        yield
        sh += 1
    return ts


def _level_dots(xs, ys):
    return [_bdot(x, y) for x, y in zip(xs, ys)]


def _delta_scan_kernel(*refs, tile, reverse, final):
    if final:
        (q_ref, k_ref, v_ref, kT_ref, gbT_ref, gbC_ref, ofwd_ref, gate_ref, onw_ref,
         o_ref, s_ref, *scratch) = refs
    else:
        q_ref, k_ref, v_ref, kT_ref, gbT_ref, gbC_ref, o_ref, s_ref, *scratch = refs
    qk_ref, n_ref, m_ref, o0_ref, qp_ref = scratch
    C = SCAN_CHUNK
    d = 1 if reverse else 0

    @pl.when(pl.program_id(1) == 0)
    def _():
        s_ref[...] = jnp.zeros_like(s_ref)

    ii = lax.broadcasted_iota(jnp.int32, (C, C), 0)
    jj = lax.broadcasted_iota(jnp.int32, (C, C), 1)
    incl = (ii <= jj) if reverse else (ii >= jj)
    strict = (ii < jj) if reverse else (ii > jj)
    last = 0 if reverse else C - 1

    n_chunks = tile // C
    order = list(range(n_chunks - 1, -1, -1) if reverse else range(n_chunks))
    items = [(c, h) for c in order for h in range(HEADS)]

    def views(c, h):
        rws = slice(c * C, (c + 1) * C)
        cols = slice(h * DK, (h + 1) * DK)
        ib = d * HEADS + h
        ig = 2 * HEADS + ib
        beta_c = gbC_ref[0, rws, ib:ib + 1]
        gc_c = gbC_ref[0, rws, ig:ig + 1]
        gc_r = gbT_ref[0, ig:ig + 1, rws]
        return rws, cols, beta_c, gc_c, gc_r

    g_tots = {}

    def phase1(group, base):
        qks, kts = [], []
        for c, h in group:
            rws, cols, _, _, _ = views(c, h)
            qks.append(jnp.concatenate([q_ref[0, rws, cols], k_ref[0, rws, cols]], axis=0))
            kts.append(kT_ref[0, cols, rws])
        qkks = _level_dots(qks, kts)
        yield
        a_list = []
        for i, (c, h) in enumerate(group):
            _, _, beta_c, gc_c, gc_r = views(c, h)
            decay = jnp.where(incl, jnp.exp(jnp.where(incl, gc_c - gc_r, 0.0)), 0.0)
            qk_ref[base + i] = (qkks[i][:C] * decay).astype(BF16)
            a_list.append(jnp.where(strict, qkks[i][C:] * decay * beta_c, 0.0))
        ts = yield from _unit_tri_inverse_steps(a_list, ii, jj, reverse)
        sols = []
        for i, (c, h) in enumerate(group):
            rws, cols, beta_c, gc_c, _ = views(c, h)
            k = k_ref[0, rws, cols]
            rhs = jnp.concatenate([v_ref[0, rws, cols] * beta_c, k * (beta_c * jnp.exp(gc_c))], axis=1)
            sols.append(_bdot(ts[i], rhs))
        yield
        kts_tail = []
        for i, (c, h) in enumerate(group):
            rws, cols, _, gc_c, gc_r = views(c, h)
            g_tot = gc_r[:, last:last + 1]
            kts_tail.append(kT_ref[0, cols, rws] * jnp.exp(g_tot - gc_r))
            g_tots[base + i] = jnp.exp(g_tot)
        sols_b = [sol.astype(BF16) for sol in sols]
        kt_uw = _level_dots(kts_tail, sols_b)
        yield
        qk_uw = _level_dots([qk_ref[base + i] for i in range(len(group))], sols_b)
        for i, (c, h) in enumerate(group):
            rws, cols, _, gc_c, _ = views(c, h)
            n_ref[base + i] = kt_uw[i][:, :DK]
            m_ref[base + i] = kt_uw[i][:, DK:].astype(BF16)
            o0_ref[base + i] = qk_uw[i][:, :DK]
            qp_ref[base + i] = (q_ref[0, rws, cols] * jnp.exp(gc_c) - qk_uw[i][:, DK:]).astype(BF16)

    half = len(items) // 2
    groups = [phase1(items[:half], 0), phase1(items[half:], half)]
    step = 0
    while groups:
        for gi, gen in enumerate(list(groups)):
            if gi == 1 and step < PHASE1_OFFSET and len(groups) == 2:
                continue
            try:
                next(gen)
            except StopIteration:
                groups.remove(gen)
        step += 1

    for ci, c in enumerate(order):
        idx = [ci * HEADS + h for h in range(HEADS)]
        ss = [s_ref[h] for h in range(HEADS)]
        ss_b = [s.astype(BF16) for s in ss]
        mss = _level_dots([m_ref[i] for i in idx], ss_b)
        for h, (i, s, ms) in enumerate(zip(idx, ss, mss)):
            s_ref[h] = s * g_tots[i] + n_ref[i] - ms
        qss = _level_dots([qp_ref[i] for i in idx], ss_b)
        os_ = [o0_ref[i] + qs for i, qs in zip(idx, qss)]
        for h, o in enumerate(os_):
            rws, cols, _, _, _ = views(c, h)
            if final:
                o = o + ofwd_ref[0, rws, cols]
                o = o * lax.rsqrt(jnp.mean(o * o, -1, keepdims=True) + NORM_EPS) * onw_ref[...]
                o_ref[0, rws, cols] = (o * gate_ref[0, rws, cols]).astype(o_ref.dtype)
            else:
                o_ref[0, rws, cols] = o


def _delta_scan(q, k, v, kT, gbT, gbC, *, reverse, final_inputs=None):
    B, L, _ = q.shape
    tile = min(SCAN_TILE, L)
    n_tiles = L // tile
    n_items = (tile // SCAN_CHUNK) * HEADS
    final = final_inputs is not None
    tix = (lambda t: n_tiles - 1 - t) if reverse else (lambda t: t)
    seq_spec = lambda w: pl.BlockSpec((1, tile, w), lambda b, t: (b, tix(t), 0))
    in_specs = [seq_spec(DN_W), seq_spec(DN_W), seq_spec(DN_W),
                pl.BlockSpec((1, DN_W, tile), lambda b, t: (b, 0, tix(t))),
                pl.BlockSpec((1, 4 * HEADS, tile), lambda b, t: (b, 0, tix(t))),
                seq_spec(SCAN_CHUNK)]
    args = [q, k, v, kT, gbT, gbC]
    if final:
        ofwd, gate, onw = final_inputs
        in_specs += [seq_spec(DN_W), seq_spec(DN_W), _const_spec(onw.shape)]
        args += [ofwd, gate, onw]
    return pl.pallas_call(
        functools.partial(_delta_scan_kernel, tile=tile, reverse=reverse, final=final),
        grid=(B, n_tiles),
        in_specs=in_specs,
        out_specs=seq_spec(DN_W),
        out_shape=jax.ShapeDtypeStruct((B, L, DN_W), BF16 if final else F32),
        scratch_shapes=[pltpu.VMEM((HEADS, DK, DK), F32),
                        pltpu.VMEM((n_items, SCAN_CHUNK, SCAN_CHUNK), BF16),
                        pltpu.VMEM((n_items, DK, DK), F32),
                        pltpu.VMEM((n_items, DK, DK), BF16),
                        pltpu.VMEM((n_items, SCAN_CHUNK, DK), F32),
                        pltpu.VMEM((n_items, SCAN_CHUNK, DK), BF16)],
        compiler_params=pltpu.CompilerParams(dimension_semantics=("parallel", "arbitrary"),
                                             vmem_limit_bytes=VMEM_LIMIT),
        name="delta_scan_bwd" if reverse else "delta_scan_fwd",
    )(*args)


def _layer_params(l, ffn1_wg, ffn1_wu, ffn1_wd, ln1_g, ln1_b, w_in, conv_w, conv_b, conv_ln_g,
                  conv_ln_b, sconv_w, a_log, dt_bias, o_norm_w, w_out, ln2_g, ln2_b,
                  ffn2_wg, ffn2_wu, ffn2_wd, ln3_g, ln3_b):
    row = lambda a: a[l].reshape(1, -1).astype(F32)
    col = lambda a: a[l].reshape(-1, 1).astype(F32)
    wi = w_in[l].astype(BF16)
    o_q = 2 * CONV_W
    o_z = o_q + 3 * DN_W
    o_b = o_z + DN_W
    wo = w_out[l].astype(BF16)
    return dict(
        ffn1=(ffn1_wg[l].astype(BF16), ffn1_wu[l].astype(BF16), ffn1_wd[l].astype(BF16),
              row(ln1_g), row(ln1_b)),
        mixer_in=(wi[:, :o_q], wi[:, o_q:o_z], wi[:, o_z:o_b], wi[:, o_b:].T,
                  conv_w[l].astype(F32), row(conv_b), row(conv_ln_g), row(conv_ln_b),
                  sconv_w[l].astype(F32), col(a_log), col(dt_bias)),
        onw=row(o_norm_w),
        mixout=(wo[:CONV_W], wo[CONV_W:], row(ln2_g), row(ln2_b),
                ffn2_wg[l].astype(BF16), ffn2_wu[l].astype(BF16), ffn2_wd[l].astype(BF16),
                row(ln3_g), row(ln3_b)),
    )


def _trunk(x, layers):
    B, L, D = x.shape
    for p in layers:
        x1 = _ffn_ln(x.reshape(B * L, D), *p["ffn1"])
        cv, q, k, v, kT, gate, gbT, gbC = _mixer_in(x1.reshape(B, L, D), *p["mixer_in"])
        o_fwd = _delta_scan(q, k, v, kT, gbT, gbC, reverse=False)
        og = _delta_scan(q, k, v, kT, gbT, gbC, reverse=True, final_inputs=(o_fwd, gate, p["onw"]))
        x = _mixout_ffn(x1, cv.reshape(B * L, CONV_W), og.reshape(B * L, DN_W), *p["mixout"])
        x = x.reshape(B, L, D)
    return x


def kernel(x_prompt, x_sample, ffn1_wg, ffn1_wu, ffn1_wd, ln1_g, ln1_b, w_in, conv_w, conv_b, conv_ln_g, conv_ln_b, sconv_w, a_log, dt_bias, o_norm_w, w_out, ln2_g, ln2_b, ffn2_wg, ffn2_wu, ffn2_wd, ln3_g, ln3_b):
    weights = (ffn1_wg, ffn1_wu, ffn1_wd, ln1_g, ln1_b, w_in, conv_w, conv_b, conv_ln_g, conv_ln_b,
               sconv_w, a_log, dt_bias, o_norm_w, w_out, ln2_g, ln2_b, ffn2_wg, ffn2_wu, ffn2_wd,
               ln3_g, ln3_b)
    layers = [_layer_params(l, *weights) for l in range(DEPTH)]
    return (_trunk(x_prompt, layers), _trunk(x_sample, layers))
```

```python
import functools

import jax
import jax.numpy as jnp
from jax import lax
from jax.experimental import pallas as pl
from jax.experimental.pallas import tpu as pltpu

F32 = jnp.float32
BF16 = jnp.bfloat16

DEPTH = 2
D_MODEL = 1024
CONV_W = 512
DN_W = 512
HEADS = 4
DK = 128
CONV_K = 31
SC_K = 5
D_FF = 2816
ALPHA = (2 * DEPTH) ** 0.25
LN_EPS = 1e-5
NORM_EPS = 1e-6

SCAN_CHUNK = 128
SUBLANES = 8
HALO = 16
TOKEN_TILE = 512
SEQ_TILE = 1024
SCAN_TILE = 512
MIX_UNIT = 256
CONV_ROWS = 32
VMEM_LIMIT = 56 * 1024 * 1024


def _dot(a, b):
    return jnp.dot(a, b, preferred_element_type=F32)


def _bdot(a, b):
    return jnp.dot(a.astype(BF16), b.astype(BF16), preferred_element_type=F32)


def _silu(x):
    return x * jax.nn.sigmoid(x)


def _layer_norm(r, g, b):
    mu = jnp.mean(r, -1, keepdims=True)
    c = r - mu
    var = jnp.mean(c * c, -1, keepdims=True)
    return c * lax.rsqrt(var + LN_EPS) * g + b


def _swiglu(x, wg_ref, wu_ref, wd_ref):
    xb = x.astype(BF16)
    hg = _dot(xb, wg_ref[...])
    hu = _dot(xb, wu_ref[...])
    a = (_silu(hg) * hu).astype(BF16)
    return _dot(a, wd_ref[...])


def _ffn_ln_kernel(x_ref, wg_ref, wu_ref, wd_ref, g_ref, b_ref, o_ref):
    x = x_ref[...]
    r = ALPHA * x + 0.5 * _swiglu(x, wg_ref, wu_ref, wd_ref)
    o_ref[...] = _layer_norm(r, g_ref[...], b_ref[...])


def _mixout_ffn_kernel(x_ref, cv_ref, og_ref, woc_ref, wod_ref, g2_ref, b2_ref,
                       wg_ref, wu_ref, wd_ref, g3_ref, b3_ref, o_ref):
    x = x_ref[...]
    m = _dot(cv_ref[...], woc_ref[...]) + _dot(og_ref[...], wod_ref[...])
    x2 = _layer_norm(ALPHA * x + m, g2_ref[...], b2_ref[...])
    r = ALPHA * x2 + 0.5 * _swiglu(x2, wg_ref, wu_ref, wd_ref)
    o_ref[...] = _layer_norm(r, g3_ref[...], b3_ref[...])


def _const_spec(shape):
    nd = len(shape)
    return pl.BlockSpec(shape, lambda *_: (0,) * nd, pipeline_mode=pl.Buffered(1))


def _row_spec(tm, width):
    return pl.BlockSpec((tm, width), lambda i: (i, 0))


def _ffn_ln(x, wg, wu, wd, g, b):
    n = x.shape[0]
    tm = min(TOKEN_TILE, n)
    return pl.pallas_call(
        _ffn_ln_kernel,
        grid=(n // tm,),
        in_specs=[_row_spec(tm, D_MODEL), _const_spec(wg.shape), _const_spec(wu.shape),
                  _const_spec(wd.shape), _const_spec(g.shape), _const_spec(b.shape)],
        out_specs=_row_spec(tm, D_MODEL),
        out_shape=jax.ShapeDtypeStruct((n, D_MODEL), F32),
        compiler_params=pltpu.CompilerParams(dimension_semantics=("parallel",),
                                             vmem_limit_bytes=VMEM_LIMIT),
        name="ffn_ln",
    )(x, wg, wu, wd, g, b)


def _mixout_ffn(x, cv, og, woc, wod, g2, b2, wg, wu, wd, g3, b3):
    n = x.shape[0]
    tm = min(TOKEN_TILE, n)
    consts = (woc, wod, g2, b2, wg, wu, wd, g3, b3)
    return pl.pallas_call(
        _mixout_ffn_kernel,
        grid=(n // tm,),
        in_specs=[_row_spec(tm, D_MODEL), _row_spec(tm, CONV_W), _row_spec(tm, DN_W)]
                 + [_const_spec(c.shape) for c in consts],
        out_specs=_row_spec(tm, D_MODEL),
        out_shape=jax.ShapeDtypeStruct((n, D_MODEL), F32),
        compiler_params=pltpu.CompilerParams(dimension_semantics=("parallel",),
                                             vmem_limit_bytes=VMEM_LIMIT),
        name="mixout_ffn",
    )(x, cv, og, *consts)


def _mixer_in_kernel(xp_ref, x_ref, xn_ref, perm_ref, permT_ref, wcvg_ref, wqkv_ref, wz_ref, wbaT_ref,
                     convw_ref, convb_ref, clng_ref, clnb_ref, sconvw_ref, alog_ref, dtb_ref,
                     cv_ref, q_ref, k_ref, v_ref, kT_ref, gate_ref, gbT_ref, gbC_ref,
                     *scratch, tile, n_tiles):
    t = pl.program_id(1)
    n_units = tile // MIX_UNIT
    xh_refs, ycat_refs, extc_refs, hc_refs, extq_refs, hq_refs = (
        scratch[i * n_units:(i + 1) * n_units] for i in range(6))
    P = MIX_UNIT // SUBLANES
    xb = x_ref[0].astype(BF16)
    for u in range(n_units):
        lo = u * MIX_UNIT
        if u == 0:
            left = jnp.where(t > 0, xp_ref[0], 0.0).astype(BF16)
        else:
            left = xb[lo - HALO:lo]
        if u == n_units - 1:
            right = jnp.where(t < n_tiles - 1, xn_ref[0], 0.0).astype(BF16)
        else:
            right = xb[lo + MIX_UNIT:lo + MIX_UNIT + HALO]
        xh_refs[u][0:HALO, :] = left
        xh_refs[u][HALO:2 * HALO, :] = right
        xh_refs[u][2 * HALO:, :] = _dot(perm_ref[...], xb[lo:lo + MIX_UNIT]).astype(BF16)

    def fill_extended(ext_ref, halo_ref, proj, reach):
        ext_ref[SUBLANES * reach:SUBLANES * reach + MIX_UNIT, :] = proj[2 * HALO:]
        halo_ref[...] = proj[0:2 * HALO]
        sub = lax.broadcasted_iota(jnp.int32, (SUBLANES, proj.shape[1]), 0)
        for m in range(-reach, 0):
            lo = SUBLANES * (m + P + reach)
            shifted = pltpu.roll(ext_ref[lo:lo + SUBLANES, :], 1, 0)
            edge = halo_ref[HALO + m:HALO + m + 1, :]
            ext_ref[SUBLANES * (m + reach):SUBLANES * (m + reach + 1), :] = jnp.where(sub == 0, edge, shifted)
        for m in range(P, P + reach):
            lo = SUBLANES * (m - P + reach)
            shifted = pltpu.roll(ext_ref[lo:lo + SUBLANES, :], SUBLANES - 1, 0)
            edge = halo_ref[HALO + m - P:HALO + m - P + 1, :]
            ext_ref[SUBLANES * (m + reach):SUBLANES * (m + reach + 1), :] = jnp.where(sub == SUBLANES - 1, edge, shifted)

    for u in range(n_units):
        xh = xh_refs[u][...]
        pc = _dot(xh, wcvg_ref[...])
        fill_extended(extc_refs[u], hc_refs[u], pc[:, :CONV_W] * jax.nn.sigmoid(pc[:, CONV_W:]), CONV_K // 2)
        fill_extended(extq_refs[u], hq_refs[u], _dot(xh, wqkv_ref[...]), SC_K // 2)

    convb = convb_ref[...]
    clng = clng_ref[...]
    clnb = clnb_ref[...]
    rows = 2 * CONV_ROWS
    for u in range(n_units):
        for r in range(MIX_UNIT // CONV_ROWS):
            acc = None
            for kk in range(CONV_K):
                lo = r * CONV_ROWS + SUBLANES * kk
                term = convw_ref[kk:kk + 1, :] * extc_refs[u][lo:lo + CONV_ROWS, :]
                acc = term if acc is None else acc + term
            c = _silu(_layer_norm(acc + convb, clng, clnb))
            ycat_refs[u][r * CONV_ROWS:(r + 1) * CONV_ROWS, 0:CONV_W] = c.astype(BF16)

        for j in range(3 * HEADS):
            cols = slice(j * DK, (j + 1) * DK)
            for r in range(MIX_UNIT // rows):
                acc = None
                for kk in range(SC_K):
                    lo = r * rows + SUBLANES * kk
                    term = sconvw_ref[kk:kk + 1, cols] * extq_refs[u][lo:lo + rows, cols]
                    acc = term if acc is None else acc + term
                y = _silu(acc)
                if j < HEADS:
                    y = y * lax.rsqrt(jnp.sum(y * y, -1, keepdims=True) + NORM_EPS) * (DK ** -0.5)
                elif j < 2 * HEADS:
                    y = y * lax.rsqrt(jnp.sum(y * y, -1, keepdims=True) + NORM_EPS)
                ycat_refs[u][r * rows:(r + 1) * rows, CONV_W + j * DK:CONV_W + (j + 1) * DK] = y.astype(BF16)

        ynat = _dot(permT_ref[...], ycat_refs[u][...])
        out_rows = slice(u * MIX_UNIT, (u + 1) * MIX_UNIT)
        cv_ref[0, out_rows, :] = ynat[:, 0:CONV_W].astype(BF16)
        q_ref[0, out_rows, :] = ynat[:, CONV_W:CONV_W + DN_W].astype(BF16)
        k_ref[0, out_rows, :] = ynat[:, CONV_W + DN_W:CONV_W + 2 * DN_W].astype(BF16)
        v_ref[0, out_rows, :] = ynat[:, CONV_W + 2 * DN_W:].astype(BF16)
        for h in range(HEADS):
            kcol = CONV_W + DN_W + h * DK
            for c in range(MIX_UNIT // SCAN_CHUNK):
                rws = slice(c * SCAN_CHUNK, (c + 1) * SCAN_CHUNK)
                orws = slice(u * MIX_UNIT + c * SCAN_CHUNK, u * MIX_UNIT + (c + 1) * SCAN_CHUNK)
                kT_ref[0, h * DK:(h + 1) * DK, orws] = ynat[rws, kcol:kcol + DK].T.astype(BF16)
    n_chunks = tile // SCAN_CHUNK

    xm = xb
    gate_ref[0] = _silu(_dot(xm, wz_ref[...]))

    pb = lax.dot_general(wbaT_ref[...], xm, (((1,), (1,)), ((), ())),
                         preferred_element_type=F32)
    beta = jax.nn.sigmoid(pb[0:2 * HEADS, :])
    a = pb[2 * HEADS:, :] + dtb_ref[...]
    softplus = jnp.maximum(a, 0.0) + jnp.log1p(jnp.exp(-jnp.abs(a)))
    g = -jnp.exp(alog_ref[...]) * softplus
    lane = lax.broadcasted_iota(jnp.int32, (2 * HEADS, SCAN_CHUNK), 1)
    fwd_row = lax.broadcasted_iota(jnp.int32, (2 * HEADS, SCAN_CHUNK), 0) < HEADS
    zpad = jnp.zeros((SCAN_CHUNK - 4 * HEADS, SCAN_CHUNK), F32)
    for c in range(n_chunks):
        cl = slice(c * SCAN_CHUNK, (c + 1) * SCAN_CHUNK)
        x = g[:, cl]
        s = 1
        while s < SCAN_CHUNK:
            pre = jnp.where(lane >= s, pltpu.roll(x, s, 1), 0.0)
            suf = jnp.where(lane < SCAN_CHUNK - s, pltpu.roll(x, SCAN_CHUNK - s, 1), 0.0)
            x = x + jnp.where(fwd_row, pre, suf)
            s *= 2
        rows16 = jnp.concatenate([beta[:, cl], x], axis=0)
        gbT_ref[0, :, cl] = rows16
        gbC_ref[0, cl, :] = jnp.concatenate([rows16, zpad], axis=0).T


def _mixer_in(x, wcvg, wqkv, wz, wbaT, convw, convb, clng, clnb, sconvw, alog, dtb):
    B, L, D = x.shape
    tile = min(SEQ_TILE, L)
    n_tiles = L // tile
    hb = tile // HALO
    n_hb = L // HALO
    unit = MIX_UNIT
    n_units = tile // unit
    r = jnp.arange(unit)
    perm = (r[None, :] == ((r // SUBLANES) + (unit // SUBLANES) * (r % SUBLANES))[:, None]).astype(BF16)
    consts = (perm, perm.T, wcvg, wqkv, wz, wbaT, convw, convb, clng, clnb, sconvw, alog, dtb)
    seq = lambda w, dt: jax.ShapeDtypeStruct((B, L, w), dt)
    seq_spec = lambda w: pl.BlockSpec((1, tile, w), lambda b, t: (b, t, 0))
    return pl.pallas_call(
        functools.partial(_mixer_in_kernel, tile=tile, n_tiles=n_tiles),
        grid=(B, n_tiles),
        in_specs=[pl.BlockSpec((1, HALO, D), lambda b, t: (b, jnp.maximum(t * hb - 1, 0), 0)),
                  pl.BlockSpec((1, tile, D), lambda b, t: (b, t, 0)),
                  pl.BlockSpec((1, HALO, D), lambda b, t: (b, jnp.minimum((t + 1) * hb, n_hb - 1), 0))]
                 + [_const_spec(c.shape) for c in consts],
        out_specs=[seq_spec(CONV_W), seq_spec(DN_W), seq_spec(DN_W), seq_spec(DN_W),
                   pl.BlockSpec((1, DN_W, tile), lambda b, t: (b, 0, t)),
                   seq_spec(DN_W),
                   pl.BlockSpec((1, 4 * HEADS, tile), lambda b, t: (b, 0, t)),
                   seq_spec(SCAN_CHUNK)],
        out_shape=[seq(CONV_W, BF16), seq(DN_W, BF16), seq(DN_W, BF16), seq(DN_W, BF16),
                   jax.ShapeDtypeStruct((B, DN_W, L), BF16),
                   seq(DN_W, F32),
                   jax.ShapeDtypeStruct((B, 4 * HEADS, L), F32),
                   seq(SCAN_CHUNK, F32)],
        scratch_shapes=[pltpu.VMEM(shape, dt) for shape, dt in (
            ((unit + 2 * HALO, D), BF16),
            ((unit, CONV_W + 3 * DN_W), BF16),
            ((unit + 2 * SUBLANES * (CONV_K // 2), CONV_W), F32),
            ((2 * HALO, CONV_W), F32),
            ((unit + 2 * SUBLANES * (SC_K // 2), 3 * DN_W), F32),
            ((2 * HALO, 3 * DN_W), F32)) for _ in range(n_units)],
        compiler_params=pltpu.CompilerParams(dimension_semantics=("parallel", "parallel"),
                                             vmem_limit_bytes=VMEM_LIMIT),
        name="mixer_in",
    )(x, x, x, *consts)


def _unit_tri_inverse_steps(a_list, ii, jj, reverse):
    eye = (ii == jj).astype(F32)
    blk = (ii >> 3) == (jj >> 3)
    ns = [jnp.where(blk, -a, 0.0) for a in a_list]
    ts = [eye + n for n in ns]
    ps = _level_dots(ns, ns)
    yield
    ts = [t + pt for t, pt in zip(ts, _level_dots(ps, ts))]
    yield
    ps = _level_dots(ps, ps)
    yield
    ts = [t + pt for t, pt in zip(ts, _level_dots(ps, ts))]
    yield
    sh = 3
    while (1 << sh) < SCAN_CHUNK:
        bs = 1 << sh
        starts = [(2 * m + (0 if reverse else 1)) * bs for m in range(SCAN_CHUNK // (2 * bs))]
        take = lambda x: jnp.concatenate([x[s:s + bs] for s in starts], axis=0)
        zero = jnp.zeros((bs, SCAN_CHUNK), F32)

        def spread(xr):
            parts = []
            for i in range(len(starts)):
                blk_rows = xr[i * bs:(i + 1) * bs]
                parts += [blk_rows, zero] if reverse else [zero, blk_rows]
            return jnp.concatenate(parts, axis=0)

        def put(x, xr):
            parts = []
            for i in range(len(starts)):
                new = xr[i * bs:(i + 1) * bs]
                keep = x[(2 * i + (1 if reverse else 0)) * bs:(2 * i + (2 if reverse else 1)) * bs]
                parts += [new, keep] if reverse else [keep, new]
            return jnp.concatenate(parts, axis=0)

        m = ((ii >> (sh + 1)) == (jj >> (sh + 1))) & ((ii >> sh) != (jj >> sh))
        ets = _level_dots([take(jnp.where(m, a, 0.0)) for a in a_list], ts)
        yield
        trs = [take(t) for t in ts]
        tets = _level_dots(trs, [spread(et) for et in ets])
        ts = [put(t, tr - tet) for t, tr, tet in zip(ts, trs, tets)]
        yield
        sh += 1
    return ts


def _level_dots(xs, ys):
    return [_bdot(x, y) for x, y in zip(xs, ys)]


def _delta_scan_kernel(*refs, tile, reverse, final):
    if final:
        (q_ref, k_ref, v_ref, kT_ref, gbT_ref, gbC_ref, ofwd_ref, gate_ref, onw_ref,
         o_ref, s_ref, *scratch) = refs
    else:
        q_ref, k_ref, v_ref, kT_ref, gbT_ref, gbC_ref, o_ref, s_ref, *scratch = refs
    qk_ref, n_ref, m_ref, o0_ref, qp_ref = scratch
    C = SCAN_CHUNK
    d = 1 if reverse else 0

    @pl.when(pl.program_id(1) == 0)
    def _():
        s_ref[...] = jnp.zeros_like(s_ref)

    ii = lax.broadcasted_iota(jnp.int32, (C, C), 0)
    jj = lax.broadcasted_iota(jnp.int32, (C, C), 1)
    incl = (ii <= jj) if reverse else (ii >= jj)
    strict = (ii < jj) if reverse else (ii > jj)
    last = 0 if reverse else C - 1

    n_chunks = tile // C
    order = list(range(n_chunks - 1, -1, -1) if reverse else range(n_chunks))
    items = [(c, h) for c in order for h in range(HEADS)]

    def views(c, h):
        rws = slice(c * C, (c + 1) * C)
        cols = slice(h * DK, (h + 1) * DK)
        ib = d * HEADS + h
        ig = 2 * HEADS + ib
        beta_c = gbC_ref[0, rws, ib:ib + 1]
        gc_c = gbC_ref[0, rws, ig:ig + 1]
        gc_r = gbT_ref[0, ig:ig + 1, rws]
        return rws, cols, beta_c, gc_c, gc_r

    g_tots = {}

    def phase1(group, base):
        qks, kts = [], []
        for c, h in group:
            rws, cols, _, _, _ = views(c, h)
            qks.append(jnp.concatenate([q_ref[0, rws, cols], k_ref[0, rws, cols]], axis=0))
            kts.append(kT_ref[0, cols, rws])
        qkks = _level_dots(qks, kts)
        yield
        a_list = []
        for i, (c, h) in enumerate(group):
            _, _, beta_c, gc_c, gc_r = views(c, h)
            decay = jnp.where(incl, jnp.exp(jnp.where(incl, gc_c - gc_r, 0.0)), 0.0)
            qk_ref[base + i] = (qkks[i][:C] * decay).astype(BF16)
            a_list.append(jnp.where(strict, qkks[i][C:] * decay * beta_c, 0.0))
        ts = yield from _unit_tri_inverse_steps(a_list, ii, jj, reverse)
        sols = []
        for i, (c, h) in enumerate(group):
            rws, cols, beta_c, gc_c, _ = views(c, h)
            k = k_ref[0, rws, cols]
            rhs = jnp.concatenate([v_ref[0, rws, cols] * beta_c, k * (beta_c * jnp.exp(gc_c))], axis=1)
            sols.append(_bdot(ts[i], rhs))
        yield
        kts_tail = []
        for i, (c, h) in enumerate(group):
            rws, cols, _, gc_c, gc_r = views(c, h)
            g_tot = gc_r[:, last:last + 1]
            kts_tail.append(kT_ref[0, cols, rws] * jnp.exp(g_tot - gc_r))
            g_tots[base + i] = jnp.exp(g_tot)
        sols_b = [sol.astype(BF16) for sol in sols]
        kt_uw = _level_dots(kts_tail, sols_b)
        yield
        qk_uw = _level_dots([qk_ref[base + i] for i in range(len(group))], sols_b)
        for i, (c, h) in enumerate(group):
            rws, cols, _, gc_c, _ = views(c, h)
            n_ref[base + i] = kt_uw[i][:, :DK]
            m_ref[base + i] = kt_uw[i][:, DK:].astype(BF16)
            o0_ref[base + i] = qk_uw[i][:, :DK]
            qp_ref[base + i] = (q_ref[0, rws, cols] * jnp.exp(gc_c) - qk_uw[i][:, DK:]).astype(BF16)

    for _ in phase1(items, 0):
        pass

    for ci, c in enumerate(order):
        idx = [ci * HEADS + h for h in range(HEADS)]
        ss = [s_ref[h] for h in range(HEADS)]
        ss_b = [s.astype(BF16) for s in ss]
        mss = _level_dots([m_ref[i] for i in idx], ss_b)
        for h, (i, s, ms) in enumerate(zip(idx, ss, mss)):
            s_ref[h] = s * g_tots[i] + n_ref[i] - ms
        qss = _level_dots([qp_ref[i] for i in idx], ss_b)
        os_ = [o0_ref[i] + qs for i, qs in zip(idx, qss)]
        for h, o in enumerate(os_):
            rws, cols, _, _, _ = views(c, h)
            if final:
                o = o + ofwd_ref[0, rws, cols]
                o = o * lax.rsqrt(jnp.mean(o * o, -1, keepdims=True) + NORM_EPS) * onw_ref[...]
                o_ref[0, rws, cols] = (o * gate_ref[0, rws, cols]).astype(o_ref.dtype)
            else:
                o_ref[0, rws, cols] = o


def _delta_scan(q, k, v, kT, gbT, gbC, *, reverse, final_inputs=None):
    B, L, _ = q.shape
    tile = min(SCAN_TILE, L)
    n_tiles = L // tile
    n_items = (tile // SCAN_CHUNK) * HEADS
    final = final_inputs is not None
    tix = (lambda t: n_tiles - 1 - t) if reverse else (lambda t: t)
    seq_spec = lambda w: pl.BlockSpec((1, tile, w), lambda b, t: (b, tix(t), 0))
    in_specs = [seq_spec(DN_W), seq_spec(DN_W), seq_spec(DN_W),
                pl.BlockSpec((1, DN_W, tile), lambda b, t: (b, 0, tix(t))),
                pl.BlockSpec((1, 4 * HEADS, tile), lambda b, t: (b, 0, tix(t))),
                seq_spec(SCAN_CHUNK)]
    args = [q, k, v, kT, gbT, gbC]
    if final:
        ofwd, gate, onw = final_inputs
        in_specs += [seq_spec(DN_W), seq_spec(DN_W), _const_spec(onw.shape)]
        args += [ofwd, gate, onw]
    return pl.pallas_call(
        functools.partial(_delta_scan_kernel, tile=tile, reverse=reverse, final=final),
        grid=(B, n_tiles),
        in_specs=in_specs,
        out_specs=seq_spec(DN_W),
        out_shape=jax.ShapeDtypeStruct((B, L, DN_W), BF16 if final else F32),
        scratch_shapes=[pltpu.VMEM((HEADS, DK, DK), F32),
                        pltpu.VMEM((n_items, SCAN_CHUNK, SCAN_CHUNK), BF16),
                        pltpu.VMEM((n_items, DK, DK), F32),
                        pltpu.VMEM((n_items, DK, DK), BF16),
                        pltpu.VMEM((n_items, SCAN_CHUNK, DK), F32),
                        pltpu.VMEM((n_items, SCAN_CHUNK, DK), BF16)],
        compiler_params=pltpu.CompilerParams(dimension_semantics=("parallel", "arbitrary"),
                                             vmem_limit_bytes=VMEM_LIMIT),
        name="delta_scan_bwd" if reverse else "delta_scan_fwd",
    )(*args)


def _layer_params(l, ffn1_wg, ffn1_wu, ffn1_wd, ln1_g, ln1_b, w_in, conv_w, conv_b, conv_ln_g,
                  conv_ln_b, sconv_w, a_log, dt_bias, o_norm_w, w_out, ln2_g, ln2_b,
                  ffn2_wg, ffn2_wu, ffn2_wd, ln3_g, ln3_b):
    row = lambda a: a[l].reshape(1, -1).astype(F32)
    col = lambda a: a[l].reshape(-1, 1).astype(F32)
    wi = w_in[l].astype(BF16)
    o_q = 2 * CONV_W
    o_z = o_q + 3 * DN_W
    o_b = o_z + DN_W
    wo = w_out[l].astype(BF16)
    return dict(
        ffn1=(ffn1_wg[l].astype(BF16), ffn1_wu[l].astype(BF16), ffn1_wd[l].astype(BF16),
              row(ln1_g), row(ln1_b)),
        mixer_in=(wi[:, :o_q], wi[:, o_q:o_z], wi[:, o_z:o_b], wi[:, o_b:].T,
                  conv_w[l].astype(F32), row(conv_b), row(conv_ln_g), row(conv_ln_b),
                  sconv_w[l].astype(F32), col(a_log), col(dt_bias)),
        onw=row(o_norm_w),
        mixout=(wo[:CONV_W], wo[CONV_W:], row(ln2_g), row(ln2_b),
                ffn2_wg[l].astype(BF16), ffn2_wu[l].astype(BF16), ffn2_wd[l].astype(BF16),
                row(ln3_g), row(ln3_b)),
    )


def _trunk(x, layers):
    B, L, D = x.shape
    for p in layers:
        x1 = _ffn_ln(x.reshape(B * L, D), *p["ffn1"])
        cv, q, k, v, kT, gate, gbT, gbC = _mixer_in(x1.reshape(B, L, D), *p["mixer_in"])
        o_fwd = _delta_scan(q, k, v, kT, gbT, gbC, reverse=False)
        og = _delta_scan(q, k, v, kT, gbT, gbC, reverse=True, final_inputs=(o_fwd, gate, p["onw"]))
        x = _mixout_ffn(x1, cv.reshape(B * L, CONV_W), og.reshape(B * L, DN_W), *p["mixout"])
        x = x.reshape(B, L, D)
    return x


def kernel(x_prompt, x_sample, ffn1_wg, ffn1_wu, ffn1_wd, ln1_g, ln1_b, w_in, conv_w, conv_b, conv_ln_g, conv_ln_b, sconv_w, a_log, dt_bias, o_norm_w, w_out, ln2_g, ln2_b, ffn2_wg, ffn2_wu, ffn2_wd, ln3_g, ln3_b):
    weights = (ffn1_wg, ffn1_wu, ffn1_wd, ln1_g, ln1_b, w_in, conv_w, conv_b, conv_ln_g, conv_ln_b,
               sconv_w, a_log, dt_bias, o_norm_w, w_out, ln2_g, ln2_b, ffn2_wg, ffn2_wu, ffn2_wd,
               ln3_g, ln3_b)
    layers = [_layer_params(l, *weights) for l in range(DEPTH)]
    return (_trunk(x_prompt, layers), _trunk(x_sample, layers))
```
